```python
import math, functools
import jax, jax.numpy as jnp
from jax import lax
import numpy as np

D_MODEL = 2048
BATCH = 2
SEQ = 4096
DEPTH = 1
DEC_BATCH = 128
DEC_SEQ = 1
PAST_LEN = 8192
PAGE_SIZE = 128

D_MIX = D_MODEL
DIFF_HEADS = 8
DIFF_KV_HEADS = 4
DIFF_GROUP = DIFF_HEADS // DIFF_KV_HEADS
DIFF_HEAD_DIM = 64
DIFF_V_DIM = 2 * DIFF_HEAD_DIM
MLA_HEADS = 8
QK_NOPE_DIM = 128
QK_ROPE_DIM = 64
V_HEAD_DIM = 128
Q_LORA_RANK = 512
KV_LORA_RANK = 256
D_FF = 5632
ROPE_THETA = 10000.0
EPS = 1e-6
Q_BLOCK = 128
MASK_VALUE = -1e30

DIFF_Q_COLS = DIFF_HEADS * 2 * DIFF_HEAD_DIM
DIFF_K_COLS = DIFF_KV_HEADS * 2 * DIFF_HEAD_DIM
DIFF_V_COLS = DIFF_KV_HEADS * DIFF_V_DIM
IN_SPLITS = (DIFF_Q_COLS,
             DIFF_Q_COLS + DIFF_K_COLS,
             DIFF_Q_COLS + DIFF_K_COLS + DIFF_V_COLS,
             DIFF_Q_COLS + DIFF_K_COLS + DIFF_V_COLS + Q_LORA_RANK,
             DIFF_Q_COLS + DIFF_K_COLS + DIFF_V_COLS + Q_LORA_RANK + KV_LORA_RANK)
IN_COLS = DIFF_Q_COLS + DIFF_K_COLS + DIFF_V_COLS + Q_LORA_RANK + KV_LORA_RANK + QK_ROPE_DIM

kernel_name = "hybrid_diffattn_mla_macaron_decode_step"


def rms_norm(x, g):
    xf = x.astype(jnp.float32)
    y = xf * lax.rsqrt(jnp.mean(xf * xf, axis=-1, keepdims=True) + EPS)
    return (y * g.astype(jnp.float32)).astype(x.dtype)


def rope(x, pos):
    dim = x.shape[-1]
    half = dim // 2
    inv = jnp.exp(-math.log(ROPE_THETA) * jnp.arange(half, dtype=jnp.float32) * (2.0 / dim))
    ang = pos[:, None] * inv[None, :]
    shape = (pos.shape[0],) + (1,) * (x.ndim - 3) + (half,)
    cos = jnp.cos(ang).reshape(shape)
    sin = jnp.sin(ang).reshape(shape)
    xf = x.astype(jnp.float32)
    x1, x2 = xf[..., :half], xf[..., half:]
    return jnp.concatenate([x1 * cos - x2 * sin, x2 * cos + x1 * sin], axis=-1).astype(x.dtype)


def swiglu(h, w_gate, w_up, w_down):
    return (jax.nn.silu(h @ w_gate) * (h @ w_up)) @ w_down


def ffn_half(x, pre, post, w_gate, w_up, w_down):
    return x + 0.5 * rms_norm(swiglu(rms_norm(x, pre), w_gate, w_up, w_down), post)


def mixer_project(h, pos, w_in, mla_q_norm, w_uq, mla_kv_norm, w_uk):
    B, T, _ = h.shape
    z = h @ w_in
    q_d, k_d, v_d, c_q, c_kv, k_pe = jnp.split(z, IN_SPLITS, axis=-1)
    q_d = rope(q_d.reshape(B, T, DIFF_KV_HEADS, DIFF_GROUP, 2, DIFF_HEAD_DIM), pos)
    k_d = rope(k_d.reshape(B, T, DIFF_KV_HEADS, 2, DIFF_HEAD_DIM), pos)
    v_d = v_d.reshape(B, T, DIFF_KV_HEADS, DIFF_V_DIM)
    q_m = (rms_norm(c_q, mla_q_norm) @ w_uq).reshape(B, T, MLA_HEADS, QK_NOPE_DIM + QK_ROPE_DIM)
    q_lat = jnp.einsum('bthn,rhn->bthr', q_m[..., :QK_NOPE_DIM], w_uk)
    q_pe = rope(q_m[..., QK_NOPE_DIM:], pos)
    c_kv = rms_norm(c_kv, mla_kv_norm)
    k_pe = rope(k_pe, pos)
    return q_d, k_d, v_d, q_lat, q_pe, c_kv, k_pe


def diff_attend(q, segments, lam, subln, lam_init):
    scale = DIFF_HEAD_DIM ** -0.5
    scores = []
    for k, _, mask in segments:
        s = jnp.einsum('btkgcd,bskcd->bkgcts', q, k).astype(jnp.float32) * scale
        if mask is not None:
            s = jnp.where(mask, s, MASK_VALUE)
        scores.append(s)
    p = jax.nn.softmax(jnp.concatenate(scores, axis=-1), axis=-1)
    a = p[:, :, :, 0] - lam * p[:, :, :, 1]
    outs = []
    off = 0
    for k, v, _ in segments:
        S = k.shape[1]
        outs.append(jnp.einsum('bkgts,bskv->btkgv', a[..., off:off + S].astype(v.dtype), v))
        off += S
    out = functools.reduce(jnp.add, outs)
    out = rms_norm(out, subln) * (1.0 - lam_init)
    B, T = q.shape[:2]
    return out.reshape(B, T, DIFF_HEADS * DIFF_V_DIM)


def mla_attend(q_lat, q_pe, segments, w_uv):
    scale = (QK_NOPE_DIM + QK_ROPE_DIM) ** -0.5
    scores = []
    for c, kp, mask in segments:
        s = (jnp.einsum('bthr,bsr->bhts', q_lat, c) + jnp.einsum('bthp,bsp->bhts', q_pe, kp)).astype(jnp.float32) * scale
        if mask is not None:
            s = jnp.where(mask, s, MASK_VALUE)
        scores.append(s)
    p = jax.nn.softmax(jnp.concatenate(scores, axis=-1), axis=-1)
    outs = []
    off = 0
    for c, _, _ in segments:
        S = c.shape[1]
        outs.append(jnp.einsum('bhts,bsr->bthr', p[..., off:off + S].astype(c.dtype), c))
        off += S
    o_lat = functools.reduce(jnp.add, outs)
    out = jnp.einsum('bthr,rhv->bthv', o_lat, w_uv)
    B, T = q_lat.shape[:2]
    return out.reshape(B, T, MLA_HEADS * V_HEAD_DIM)


def prompt_mix(q_d, k_d, v_d, q_lat, q_pe, c_kv, k_pe, lam, subln, lam_init, w_uv):
    B, T = q_d.shape[:2]
    key_pos = jnp.arange(T)

    def block(i):
        start = i * Q_BLOCK
        sl = lambda a: lax.dynamic_slice_in_dim(a, start, Q_BLOCK, axis=1)
        mask = (start + jnp.arange(Q_BLOCK))[:, None] >= key_pos[None, :]
        od = diff_attend(sl(q_d), [(k_d, v_d, mask)], lam, subln, lam_init)
        om = mla_attend(sl(q_lat), sl(q_pe), [(c_kv, k_pe, mask)], w_uv)
        return jnp.concatenate([od, om], axis=-1)

    mixed = lax.map(block, jnp.arange(T // Q_BLOCK))
    return jnp.moveaxis(mixed, 0, 1).reshape(B, T, D_MIX)


def sample_mix(q_d, k_d, v_d, q_lat, q_pe, c_kv, k_pe, past_kd, past_vd, past_ckv, past_kpe,
               lam, subln, lam_init, w_uv):
    T = q_d.shape[1]
    mask = jnp.tril(jnp.ones((T, T), dtype=bool))
    od = diff_attend(q_d, [(past_kd, past_vd, None), (k_d, v_d, mask)], lam, subln, lam_init)
    om = mla_attend(q_lat, q_pe, [(past_ckv, past_kpe, None), (c_kv, k_pe, mask)], w_uv)
    return jnp.concatenate([od, om], axis=-1)


def setup_inputs(seed: int = 0) -> dict:
    key = jax.random.key(seed)
    ks = iter(jax.random.split(key, 40))
    nrm = lambda shape, scale: jax.random.normal(next(ks), shape, jnp.float32) * scale
    gain = lambda dim: 1.0 + 0.01 * jax.random.normal(next(ks), (DEPTH, dim), jnp.float32)
    n_pages = PAST_LEN // PAGE_SIZE
    n_used = DEC_BATCH * n_pages
    n_phys = n_used + max(1, n_used // 4)
    inp = {}
    inp['x_prompt'] = nrm((BATCH, SEQ, D_MODEL), 1.0)
    inp['x_sample'] = nrm((DEC_BATCH, DEC_SEQ, D_MODEL), 1.0)
    inp['cache_diff_k'] = nrm((DEPTH, n_phys, PAGE_SIZE, DIFF_KV_HEADS, 2, DIFF_HEAD_DIM), 1.0)
    inp['cache_diff_v'] = nrm((DEPTH, n_phys, PAGE_SIZE, DIFF_KV_HEADS, DIFF_V_DIM), 1.0)
    inp['cache_mla_ckv'] = nrm((DEPTH, n_phys, PAGE_SIZE, KV_LORA_RANK), 1.0)
    inp['cache_mla_kpe'] = nrm((DEPTH, n_phys, PAGE_SIZE, QK_ROPE_DIM), 1.0)
    perm = jax.random.permutation(next(ks), n_phys)
    inp['page_table'] = perm[:n_used].reshape(DEC_BATCH, n_pages).astype(jnp.int32)
    inp['ln_ffn1_pre'] = gain(D_MODEL)
    inp['ln_ffn1_post'] = gain(D_MODEL)
    inp['ffn1_w_gate'] = nrm((DEPTH, D_MODEL, D_FF), D_MODEL ** -0.5)
    inp['ffn1_w_up'] = nrm((DEPTH, D_MODEL, D_FF), D_MODEL ** -0.5)
    inp['ffn1_w_down'] = nrm((DEPTH, D_FF, D_MODEL), D_FF ** -0.5)
    inp['ln_mix_pre'] = gain(D_MODEL)
    inp['ln_mix_post'] = gain(D_MODEL)
    inp['w_in'] = nrm((DEPTH, D_MODEL, IN_COLS), D_MODEL ** -0.5)
    inp['diff_lambda_q1'] = nrm((DEPTH, DIFF_HEAD_DIM), 0.1)
    inp['diff_lambda_k1'] = nrm((DEPTH, DIFF_HEAD_DIM), 0.1)
    inp['diff_lambda_q2'] = nrm((DEPTH, DIFF_HEAD_DIM), 0.1)
    inp['diff_lambda_k2'] = nrm((DEPTH, DIFF_HEAD_DIM), 0.1)
    inp['diff_subln'] = gain(DIFF_V_DIM)
    inp['mla_q_norm'] = gain(Q_LORA_RANK)
    inp['w_uq'] = nrm((DEPTH, Q_LORA_RANK, MLA_HEADS * (QK_NOPE_DIM + QK_ROPE_DIM)), Q_LORA_RANK ** -0.5)
    inp['mla_kv_norm'] = gain(KV_LORA_RANK)
    inp['w_uk'] = nrm((DEPTH, KV_LORA_RANK, MLA_HEADS, QK_NOPE_DIM), KV_LORA_RANK ** -0.5)
    inp['w_uv'] = nrm((DEPTH, KV_LORA_RANK, MLA_HEADS, V_HEAD_DIM), KV_LORA_RANK ** -0.5)
    inp['w_o'] = nrm((DEPTH, D_MIX, D_MODEL), D_MIX ** -0.5)
    inp['ln_ffn2_pre'] = gain(D_MODEL)
    inp['ln_ffn2_post'] = gain(D_MODEL)
    inp['ffn2_w_gate'] = nrm((DEPTH, D_MODEL, D_FF), D_MODEL ** -0.5)
    inp['ffn2_w_up'] = nrm((DEPTH, D_MODEL, D_FF), D_MODEL ** -0.5)
    inp['ffn2_w_down'] = nrm((DEPTH, D_FF, D_MODEL), D_FF ** -0.5)
    return inp


def reference(x_prompt, x_sample, cache_diff_k, cache_diff_v, cache_mla_ckv, cache_mla_kpe, page_table,
              ln_ffn1_pre, ln_ffn1_post, ffn1_w_gate, ffn1_w_up, ffn1_w_down,
              ln_mix_pre, ln_mix_post, w_in, diff_lambda_q1, diff_lambda_k1, diff_lambda_q2, diff_lambda_k2,
              diff_subln, mla_q_norm, w_uq, mla_kv_norm, w_uk, w_uv, w_o,
              ln_ffn2_pre, ln_ffn2_post, ffn2_w_gate, ffn2_w_up, ffn2_w_down):
    dec_b, n_pages = page_table.shape
    past_len = n_pages * PAGE_SIZE
    t_p = x_prompt.shape[1]
    t_s = x_sample.shape[1]
    pos_p = jnp.arange(t_p, dtype=jnp.float32)
    pos_s = jnp.arange(t_s, dtype=jnp.float32) + past_len
    xp, xs = x_prompt, x_sample
    nkp, nvp, ncp, npp, nks, nvs, ncs, nps = [], [], [], [], [], [], [], []
    for l in range(DEPTH):
        lam_init = 0.8 - 0.6 * math.exp(-0.3 * l)
        lam = (jnp.exp(jnp.sum(diff_lambda_q1[l].astype(jnp.float32) * diff_lambda_k1[l].astype(jnp.float32)))
               - jnp.exp(jnp.sum(diff_lambda_q2[l].astype(jnp.float32) * diff_lambda_k2[l].astype(jnp.float32)))
               + lam_init)
        proj_w = (w_in[l], mla_q_norm[l], w_uq[l], mla_kv_norm[l], w_uk[l])
        xp = ffn_half(xp, ln_ffn1_pre[l], ln_ffn1_post[l], ffn1_w_gate[l], ffn1_w_up[l], ffn1_w_down[l])
        xs = ffn_half(xs, ln_ffn1_pre[l], ln_ffn1_post[l], ffn1_w_gate[l], ffn1_w_up[l], ffn1_w_down[l])
        qd, kd, vd, ql, qp, ckv, kpe = mixer_project(rms_norm(xp, ln_mix_pre[l]), pos_p, *proj_w)
        mixed = prompt_mix(qd, kd, vd, ql, qp, ckv, kpe, lam, diff_subln[l], lam_init, w_uv[l])
        xp = xp + rms_norm(mixed @ w_o[l], ln_mix_post[l])
        nkp.append(kd); nvp.append(vd); ncp.append(ckv); npp.append(kpe)
        past_kd = cache_diff_k[l, page_table].reshape(dec_b, past_len, DIFF_KV_HEADS, 2, DIFF_HEAD_DIM)
        past_vd = cache_diff_v[l, page_table].reshape(dec_b, past_len, DIFF_KV_HEADS, DIFF_V_DIM)
        past_ckv = cache_mla_ckv[l, page_table].reshape(dec_b, past_len, KV_LORA_RANK)
        past_kpe = cache_mla_kpe[l, page_table].reshape(dec_b, past_len, QK_ROPE_DIM)
        qd, kd, vd, ql, qp, ckv, kpe = mixer_project(rms_norm(xs, ln_mix_pre[l]), pos_s, *proj_w)
        mixed = sample_mix(qd, kd, vd, ql, qp, ckv, kpe, past_kd, past_vd, past_ckv, past_kpe,
                           lam, diff_subln[l], lam_init, w_uv[l])
        xs = xs + rms_norm(mixed @ w_o[l], ln_mix_post[l])
        nks.append(kd); nvs.append(vd); ncs.append(ckv); nps.append(kpe)
        xp = ffn_half(xp, ln_ffn2_pre[l], ln_ffn2_post[l], ffn2_w_gate[l], ffn2_w_up[l], ffn2_w_down[l])
        xs = ffn_half(xs, ln_ffn2_pre[l], ln_ffn2_post[l], ffn2_w_gate[l], ffn2_w_up[l], ffn2_w_down[l])
    return (xp, xs, jnp.stack(nkp), jnp.stack(nvp), jnp.stack(ncp), jnp.stack(npp),
            jnp.stack(nks), jnp.stack(nvs), jnp.stack(ncs), jnp.stack(nps))
```

```python
import functools
import math

import jax
import jax.numpy as jnp
from jax import lax
from jax.experimental import pallas as pl
from jax.experimental.pallas import tpu as pltpu

F32 = jnp.float32
BF16 = jnp.bfloat16

EPS = 1e-6
ROPE_THETA = 10000.0
MASK_VALUE = -1e30

DIFF_HEADS = 8
DIFF_KV_HEADS = 4
DIFF_GROUP = DIFF_HEADS // DIFF_KV_HEADS
DIFF_HEAD_DIM = 64
DIFF_V_DIM = 2 * DIFF_HEAD_DIM
MLA_HEADS = 8
QK_NOPE_DIM = 128
QK_ROPE_DIM = 64
V_HEAD_DIM = 128
Q_LORA_RANK = 512
KV_LORA_RANK = 256

DIFF_Q_COLS = DIFF_HEADS * 2 * DIFF_HEAD_DIM
DIFF_K_COLS = DIFF_KV_HEADS * 2 * DIFF_HEAD_DIM
DIFF_V_COLS = DIFF_KV_HEADS * DIFF_V_DIM
MLA_KEY_COLS = KV_LORA_RANK + 2 * QK_ROPE_DIM

LANES = 128
V7X_VMEM_LIMIT = 56 * 1024 * 1024

_NT = (((1,), (1,)), ((), ()))


def _dot(a, b):
    return jnp.dot(a, b, preferred_element_type=F32)


def _dot_nt(a, b):
    return lax.dot_general(a, b, _NT, preferred_element_type=F32)


def _rms(x, g):
    return x * lax.rsqrt(jnp.mean(x * x, axis=-1, keepdims=True) + EPS) * g


def _params(sem, vmem=V7X_VMEM_LIMIT):
    return pltpu.CompilerParams(dimension_semantics=sem, vmem_limit_bytes=vmem)


def _ffn_body(x_ref, pre_ref, post_ref, wg_ref, wu_ref, wd_ref, o_ref, h_ref, acc_ref, *, nf):
    f = pl.program_id(1)

    @pl.when(f == 0)
    def _():
        h_ref[...] = _rms(x_ref[...], pre_ref[...]).astype(BF16)
        acc_ref[...] = jnp.zeros_like(acc_ref)

    h = h_ref[...]
    g = _dot(h, wg_ref[...])
    u = _dot(h, wu_ref[...])
    a = (g / (1.0 + jnp.exp(-g))) * u
    acc_ref[...] += _dot(a.astype(BF16), wd_ref[...])

    @pl.when(f == nf - 1)
    def _():
        o_ref[...] = x_ref[...] + 0.5 * _rms(acc_ref[...], post_ref[...])


def _ffn(x, pre, post, wg, wu, wd, *, tm, tf):
    m, d = x.shape
    nf = wg.shape[1] // tf
    row = lambda i, f: (i, 0)
    const = lambda i, f: (0, 0)
    return pl.pallas_call(
        functools.partial(_ffn_body, nf=nf),
        out_shape=jax.ShapeDtypeStruct((m, d), F32),
        grid=(m // tm, nf),
        in_specs=[pl.BlockSpec((tm, d), row),
                  pl.BlockSpec((1, d), const),
                  pl.BlockSpec((1, d), const),
                  pl.BlockSpec((d, tf), lambda i, f: (0, f)),
                  pl.BlockSpec((d, tf), lambda i, f: (0, f)),
                  pl.BlockSpec((tf, d), lambda i, f: (f, 0))],
        out_specs=pl.BlockSpec((tm, d), row),
        scratch_shapes=[pltpu.VMEM((tm, d), BF16), pltpu.VMEM((tm, d), F32)],
        compiler_params=_params(("parallel", "arbitrary")),
        name="ffn",
    )(x, pre, post, wg, wu, wd)


def _proj_body(x_ref, g_ref, win_ref, qn_ref, wuq_ref, kvn_ref, wukt_ref, cos_ref, sin_ref,
               qd_ref, kdf_ref, kdb_ref, vdf_ref, vdb_ref, ckv_ref, kpe_ref, kvm_ref, qm_ref):
    h = _rms(x_ref[...], g_ref[...]).astype(BF16)
    cos = cos_ref[...]
    sin = sin_ref[...]
    lane = lax.broadcasted_iota(jnp.int32, cos.shape, 1)
    first_half = (lane & (DIFF_HEAD_DIM - 1)) < DIFF_HEAD_DIM // 2

    def rope(z):
        swapped = jnp.where(first_half, pltpu.roll(z, LANES - 32, 1), pltpu.roll(z, 32, 1))
        return z * cos + swapped * sin

    c0 = 0
    zq = _dot(h, win_ref[:, c0:c0 + DIFF_Q_COLS])
    diff_scale = DIFF_HEAD_DIM ** -0.5
    for j in range(DIFF_Q_COLS // LANES):
        sl = slice(j * LANES, (j + 1) * LANES)
        qd_ref[:, sl] = (rope(zq[:, sl]) * diff_scale).astype(BF16)
    c0 += DIFF_Q_COLS

    zk = _dot(h, win_ref[:, c0:c0 + DIFF_K_COLS])
    for j in range(DIFF_K_COLS // LANES):
        sl = slice(j * LANES, (j + 1) * LANES)
        r = rope(zk[:, sl])
        kdf_ref[:, sl] = r
        kdb_ref[:, sl] = r.astype(BF16)
    c0 += DIFF_K_COLS

    zv = _dot(h, win_ref[:, c0:c0 + DIFF_V_COLS])
    vdf_ref[...] = zv
    vdb_ref[...] = zv.astype(BF16)
    c0 += DIFF_V_COLS

    cq = _rms(_dot(h, win_ref[:, c0:c0 + Q_LORA_RANK]), qn_ref[...]).astype(BF16)
    c0 += Q_LORA_RANK
    qm = _dot(cq, wuq_ref[...])
    mla_scale = (QK_NOPE_DIM + QK_ROPE_DIM) ** -0.5
    pe0 = MLA_HEADS * QK_NOPE_DIM
    for hh in range(MLA_HEADS):
        q_nope = qm[:, hh * QK_NOPE_DIM:(hh + 1) * QK_NOPE_DIM].astype(BF16)
        q_lat = _dot(q_nope, wukt_ref[hh])
        qm_ref[hh, :, 0:KV_LORA_RANK] = (q_lat * mla_scale).astype(BF16)
        q_pe = rope(qm[:, pe0 + hh * LANES:pe0 + (hh + 1) * LANES])
        qm_ref[hh, :, KV_LORA_RANK:MLA_KEY_COLS] = (q_pe * mla_scale).astype(BF16)

    ckv = _rms(_dot(h, win_ref[:, c0:c0 + KV_LORA_RANK]), kvn_ref[...])
    ckv_ref[...] = ckv
    kvm_ref[:, 0:KV_LORA_RANK] = ckv.astype(BF16)
    c0 += KV_LORA_RANK

    kpe = rope(_dot(h, win_ref[:, c0:c0 + LANES]))
    kpe_ref[...] = kpe[:, 0:QK_ROPE_DIM]
    kvm_ref[:, KV_LORA_RANK:MLA_KEY_COLS] = kpe.astype(BF16)


def _proj(x, gain, w_in, q_norm, w_uq, kv_norm, w_ukt, cos, sin, *, tm):
    m, d = x.shape
    row = lambda i: (i, 0)
    const2 = lambda i: (0, 0)
    const3 = lambda i: (0, 0, 0)
    out_shape = (
        jax.ShapeDtypeStruct((m, DIFF_Q_COLS), BF16),
        jax.ShapeDtypeStruct((m, DIFF_K_COLS), F32),
        jax.ShapeDtypeStruct((m, DIFF_K_COLS), BF16),
        jax.ShapeDtypeStruct((m, DIFF_V_COLS), F32),
        jax.ShapeDtypeStruct((m, DIFF_V_COLS), BF16),
        jax.ShapeDtypeStruct((m, KV_LORA_RANK), F32),
        jax.ShapeDtypeStruct((m, QK_ROPE_DIM), F32),
        jax.ShapeDtypeStruct((m, MLA_KEY_COLS), BF16),
        jax.ShapeDtypeStruct((MLA_HEADS, m, MLA_KEY_COLS), BF16),
    )
    out_specs = (
        pl.BlockSpec((tm, DIFF_Q_COLS), row),
        pl.BlockSpec((tm, DIFF_K_COLS), row),
        pl.BlockSpec((tm, DIFF_K_COLS), row),
        pl.BlockSpec((tm, DIFF_V_COLS), row),
        pl.BlockSpec((tm, DIFF_V_COLS), row),
        pl.BlockSpec((tm, KV_LORA_RANK), row),
        pl.BlockSpec((tm, QK_ROPE_DIM), row),
        pl.BlockSpec((tm, MLA_KEY_COLS), row),
        pl.BlockSpec((MLA_HEADS, tm, MLA_KEY_COLS), lambda i: (0, i, 0)),
    )
    return pl.pallas_call(
        _proj_body,
        out_shape=out_shape,
        grid=(m // tm,),
        in_specs=[pl.BlockSpec((tm, d), row),
                  pl.BlockSpec((1, d), const2),
                  pl.BlockSpec(w_in.shape, const2),
                  pl.BlockSpec((1, Q_LORA_RANK), const2),
                  pl.BlockSpec(w_uq.shape, const2),
                  pl.BlockSpec((1, KV_LORA_RANK), const2),
                  pl.BlockSpec(w_ukt.shape, const3),
                  pl.BlockSpec((tm, LANES), row),
                  pl.BlockSpec((tm, LANES), row)],
        out_specs=out_specs,
        compiler_params=_params(("parallel",)),
        name="proj",
    )(x, gain, w_in, q_norm, w_uq, kv_norm, w_ukt, cos, sin)


def _diff_lambda(lq1, lk1, lq2, lk2, lam_init):
    a = jnp.exp(jnp.sum(lq1[...] * lk1[...], axis=-1, keepdims=True))
    b = jnp.exp(jnp.sum(lq2[...] * lk2[...], axis=-1, keepdims=True))
    return a - b + lam_init


def _online_update(s, m_ref, l_ref):
    m_prev = m_ref[...]
    m_new = jnp.maximum(m_prev, jnp.max(s, axis=-1, keepdims=True))
    alpha = jnp.exp(m_prev - m_new)
    p = jnp.exp(s - m_new)
    l_ref[...] = alpha * l_ref[...] + jnp.sum(p, axis=-1, keepdims=True)
    m_ref[...] = m_new
    return alpha, p


def _diff_flash_body(q_ref, k_ref, v_ref, lq1, lk1, lq2, lk2, subln_ref, o_ref,
                     m_ref, l_ref, acc_ref, *, tq, lam_init):
    qi = pl.program_id(2)
    q = q_ref[0]
    lane = lax.broadcasted_iota(jnp.int32, (tq, LANES), 1)
    comp0 = lane < DIFF_HEAD_DIM
    qa, qb = q[:, 0:LANES], q[:, LANES:2 * LANES]
    zero = jnp.zeros_like(qa)
    q4 = jnp.concatenate([jnp.where(comp0, qa, zero), jnp.where(comp0, qb, zero),
                          jnp.where(comp0, zero, qa), jnp.where(comp0, zero, qb)], axis=0)

    m_ref[...] = jnp.full_like(m_ref, MASK_VALUE)
    l_ref[...] = jnp.zeros_like(l_ref)
    acc_ref[...] = jnp.zeros_like(acc_ref)

    def step(j, masked):
        start = pl.multiple_of(j * tq, tq)
        k = k_ref[0, pl.ds(start, tq), :]
        v = v_ref[0, pl.ds(start, tq), :]
        s = _dot_nt(q4, k)
        if masked:
            row = lax.broadcasted_iota(jnp.int32, s.shape, 0) & (tq - 1)
            col = lax.broadcasted_iota(jnp.int32, s.shape, 1)
            s = jnp.where(row >= col, s, MASK_VALUE)
        alpha, p = _online_update(s, m_ref, l_ref)
        acc_ref[...] = alpha * acc_ref[...] + _dot(p.astype(BF16), v)

    def body(j, carry):
        step(j, False)
        return carry

    lax.fori_loop(0, qi, body, 0)
    step(qi, True)

    o = acc_ref[...] / l_ref[...]
    lam = _diff_lambda(lq1, lk1, lq2, lk2, lam_init)
    a = o[0:2 * tq] - lam * o[2 * tq:4 * tq]
    y = _rms(a, subln_ref[...]) * (1.0 - lam_init)
    o_ref[0, :, 0:LANES] = y[0:tq].astype(BF16)
    o_ref[0, :, LANES:2 * LANES] = y[tq:2 * tq].astype(BF16)


def _diff_flash(qd, kd, vd, lq1, lk1, lq2, lk2, subln, *, tq, lam_init):
    b, t, _ = qd.shape
    const = lambda bi, ki, qi: (0, 0)
    lam_spec = pl.BlockSpec((1, DIFF_HEAD_DIM), const)
    return pl.pallas_call(
        functools.partial(_diff_flash_body, tq=tq, lam_init=lam_init),
        out_shape=jax.ShapeDtypeStruct((b, t, DIFF_HEADS * DIFF_V_DIM), BF16),
        grid=(b, DIFF_KV_HEADS, t // tq),
        in_specs=[pl.BlockSpec((1, tq, 2 * LANES), lambda bi, ki, qi: (bi, qi, ki)),
                  pl.BlockSpec((1, t, LANES), lambda bi, ki, qi: (bi, 0, ki)),
                  pl.BlockSpec((1, t, LANES), lambda bi, ki, qi: (bi, 0, ki)),
                  lam_spec, lam_spec, lam_spec, lam_spec,
                  pl.BlockSpec((1, DIFF_V_DIM), const)],
        out_specs=pl.BlockSpec((1, tq, 2 * LANES), lambda bi, ki, qi: (bi, qi, ki)),
        scratch_shapes=[pltpu.VMEM((4 * tq, 1), F32), pltpu.VMEM((4 * tq, 1), F32),
                        pltpu.VMEM((4 * tq, DIFF_V_DIM), F32)],
        compiler_params=_params(("parallel", "parallel", "arbitrary")),
        name="diff_flash",
    )(qd, kd, vd, lq1, lk1, lq2, lk2, subln)


def _mla_flash_body(q_ref, kv_ref, wuv_ref, o_ref, m_ref, l_ref, acc_ref, *, tq, tk):
    qi = pl.program_id(1)
    q = jnp.concatenate([q_ref[hh] for hh in range(MLA_HEADS)], axis=0)

    m_ref[...] = jnp.full_like(m_ref, MASK_VALUE)
    l_ref[...] = jnp.zeros_like(l_ref)
    acc_ref[...] = jnp.zeros_like(acc_ref)

    def step(j, masked):
        start = pl.multiple_of(j * tk, tk)
        kv = kv_ref[0, pl.ds(start, tk), :]
        s = _dot_nt(q, kv)
        if masked:
            row = (lax.broadcasted_iota(jnp.int32, s.shape, 0) & (tq - 1)) + qi * tq
            col = lax.broadcasted_iota(jnp.int32, s.shape, 1) + start
            s = jnp.where(row >= col, s, MASK_VALUE)
        alpha, p = _online_update(s, m_ref, l_ref)
        acc_ref[...] = alpha * acc_ref[...] + _dot(p.astype(BF16), kv[:, 0:KV_LORA_RANK])

    def body(j, carry):
        step(j, False)
        return carry

    n_full = (qi * tq) // tk
    lax.fori_loop(0, n_full, body, 0)
    step(n_full, True)

    o = (acc_ref[...] / l_ref[...]).astype(BF16)
    for hh in range(MLA_HEADS):
        o_ref[0, :, hh * V_HEAD_DIM:(hh + 1) * V_HEAD_DIM] = _dot(
            o[hh * tq:(hh + 1) * tq], wuv_ref[hh]).astype(BF16)


def _mla_flash(qm, kvm, w_uvh, *, tq, tk):
    b, t, _ = kvm.shape
    nq = t // tq
    return pl.pallas_call(
        functools.partial(_mla_flash_body, tq=tq, tk=tk),
        out_shape=jax.ShapeDtypeStruct((b, t, MLA_HEADS * V_HEAD_DIM), BF16),
        grid=(b, nq),
        in_specs=[pl.BlockSpec((MLA_HEADS, tq, MLA_KEY_COLS), lambda bi, qi: (0, bi * nq + qi, 0)),
                  pl.BlockSpec((1, t, MLA_KEY_COLS), lambda bi, qi: (bi, 0, 0)),
                  pl.BlockSpec(w_uvh.shape, lambda bi, qi: (0, 0, 0))],
        out_specs=pl.BlockSpec((1, tq, MLA_HEADS * V_HEAD_DIM), lambda bi, qi: (bi, qi, 0)),
        scratch_shapes=[pltpu.VMEM((MLA_HEADS * tq, 1), F32), pltpu.VMEM((MLA_HEADS * tq, 1), F32),
                        pltpu.VMEM((MLA_HEADS * tq, KV_LORA_RANK), F32)],
        compiler_params=_params(("parallel", "arbitrary")),
        name="mla_flash",
    )(qm, kvm, w_uvh)


def _decode_body(pt_ref, qbd_ref, qm_ref, knew_ref, vnew_ref, cnew_ref, pnew_ref,
                 lq1, lk1, lq2, lk2, subln_ref, *rest, pages, lam_init):
    del pt_ref
    k_refs = rest[0:pages]
    v_refs = rest[pages:2 * pages]
    c_refs = rest[2 * pages:3 * pages]
    p_refs = rest[3 * pages:4 * pages]
    od_ref, om_ref, md_ref, ld_ref, accd_ref, mm_ref, lm_ref, accm_ref = rest[4 * pages:]
    step = pl.program_id(1)
    n_rows_d = 2 * DIFF_HEADS

    @pl.when(step == 0)
    def _():
        md_ref[...] = jnp.full_like(md_ref, MASK_VALUE)
        ld_ref[...] = jnp.zeros_like(ld_ref)
        accd_ref[...] = jnp.zeros_like(accd_ref)
        mm_ref[...] = jnp.full_like(mm_ref, MASK_VALUE)
        lm_ref[...] = jnp.zeros_like(lm_ref)
        accm_ref[...] = jnp.zeros_like(accm_ref)

    qbd = qbd_ref[0]
    qm = qm_ref[0]
    q_lat = qm[:, 0:KV_LORA_RANK]
    q_pe = qm[:, KV_LORA_RANK:KV_LORA_RANK + QK_ROPE_DIM]

    sd = jnp.concatenate([_dot_nt(qbd, k_refs[i][...].astype(BF16)) for i in range(pages)], axis=1)
    alpha, p = _online_update(sd, md_ref, ld_ref)
    p = p.astype(BF16)
    pv = _dot(p[:, 0:LANES], v_refs[0][...].astype(BF16))
    for i in range(1, pages):
        pv += _dot(p[:, i * LANES:(i + 1) * LANES], v_refs[i][...].astype(BF16))
    accd_ref[...] = alpha * accd_ref[...] + pv

    cs = [c_refs[i][...].astype(BF16) for i in range(pages)]
    sm = jnp.concatenate([_dot_nt(q_lat, cs[i]) + _dot_nt(q_pe, p_refs[i][...].astype(BF16))
                          for i in range(pages)], axis=1)
    alpha, p = _online_update(sm, mm_ref, lm_ref)
    p = p.astype(BF16)
    pv = _dot(p[:, 0:LANES], cs[0])
    for i in range(1, pages):
        pv += _dot(p[:, i * LANES:(i + 1) * LANES], cs[i])
    accm_ref[...] = alpha * accm_ref[...] + pv

    @pl.when(step == pl.num_programs(1) - 1)
    def _():
        round_bf16 = lambda a: a.astype(BF16).astype(F32)
        k_new = round_bf16(knew_ref[0])
        v_new = round_bf16(vnew_ref[0])
        s_new = jnp.sum(qbd.astype(F32) * k_new, axis=-1, keepdims=True)
        alpha, p_new = _online_update(s_new, md_ref, ld_ref)
        acc = alpha * accd_ref[...] + round_bf16(p_new) * v_new
        row_head = (lax.broadcasted_iota(jnp.int32, acc.shape, 0) & (DIFF_HEADS - 1)) // DIFF_GROUP
        col_head = lax.broadcasted_iota(jnp.int32, acc.shape, 1) // DIFF_V_DIM
        own = jnp.where(row_head == col_head, acc, 0.0)
        o = own[:, 0:DIFF_V_DIM]
        for kk in range(1, DIFF_KV_HEADS):
            o += own[:, kk * DIFF_V_DIM:(kk + 1) * DIFF_V_DIM]
        o = o / ld_ref[...]
        lam = _diff_lambda(lq1, lk1, lq2, lk2, lam_init)
        a = o[0:DIFF_HEADS] - lam * o[DIFF_HEADS:n_rows_d]
        od_ref[0] = _rms(a, subln_ref[...]) * (1.0 - lam_init)

        c_new = round_bf16(cnew_ref[0])
        pe_new = round_bf16(pnew_ref[0])
        s_new = (jnp.sum(q_lat.astype(F32) * c_new, axis=-1, keepdims=True)
                 + jnp.sum(q_pe.astype(F32) * pe_new, axis=-1, keepdims=True))
        alpha, p_new = _online_update(s_new, mm_ref, lm_ref)
        om_ref[0] = (alpha * accm_ref[...] + round_bf16(p_new) * c_new) / lm_ref[...]


def _decode(page_table, qbd, qm, k_new, v_new, c_new, pe_new, lq1, lk1, lq2, lk2, subln,
            cache_k, cache_v, cache_c, cache_p, *, pages, lam_init):
    nb, n_pages = page_table.shape
    page = cache_k.shape[1]
    per_sample = lambda b, s, pt: (b, 0, 0)
    const = lambda b, s, pt: (0, 0)
    lam_spec = pl.BlockSpec((1, DIFF_HEAD_DIM), const)

    def paged(width):
        return [pl.BlockSpec((None, page, width),
                             functools.partial(lambda b, s, pt, i: (pt[b, s * pages + i], 0, 0), i=i))
                for i in range(pages)]

    n_rows_d = 2 * DIFF_HEADS
    grid_spec = pltpu.PrefetchScalarGridSpec(
        num_scalar_prefetch=1,
        grid=(nb, n_pages // pages),
        in_specs=[pl.BlockSpec((1, n_rows_d, DIFF_K_COLS), per_sample),
                  pl.BlockSpec((1, MLA_HEADS, MLA_KEY_COLS), per_sample),
                  pl.BlockSpec((1, 1, DIFF_K_COLS), per_sample),
                  pl.BlockSpec((1, 1, DIFF_V_COLS), per_sample),
                  pl.BlockSpec((1, 1, KV_LORA_RANK), per_sample),
                  pl.BlockSpec((1, 1, QK_ROPE_DIM), per_sample),
                  lam_spec, lam_spec, lam_spec, lam_spec,
                  pl.BlockSpec((1, DIFF_V_DIM), const)]
                 + paged(DIFF_K_COLS) + paged(DIFF_V_COLS) + paged(KV_LORA_RANK) + paged(QK_ROPE_DIM),
        out_specs=(pl.BlockSpec((1, DIFF_HEADS, DIFF_V_DIM), per_sample),
                   pl.BlockSpec((1, MLA_HEADS, KV_LORA_RANK), per_sample)),
        scratch_shapes=[pltpu.VMEM((n_rows_d, 1), F32), pltpu.VMEM((n_rows_d, 1), F32),
                        pltpu.VMEM((n_rows_d, DIFF_V_COLS), F32),
                        pltpu.VMEM((MLA_HEADS, 1), F32), pltpu.VMEM((MLA_HEADS, 1), F32),
                        pltpu.VMEM((MLA_HEADS, KV_LORA_RANK), F32)],
    )
    return pl.pallas_call(
        functools.partial(_decode_body, pages=pages, lam_init=lam_init),
        out_shape=(jax.ShapeDtypeStruct((nb, DIFF_HEADS, DIFF_V_DIM), F32),
                   jax.ShapeDtypeStruct((nb, MLA_HEADS, KV_LORA_RANK), F32)),
        grid_spec=grid_spec,
        compiler_params=_params(("parallel", "arbitrary")),
        name="decode",
    )(page_table, qbd, qm, k_new, v_new, c_new, pe_new, lq1, lk1, lq2, lk2, subln,
      *([cache_k] * pages), *([cache_v] * pages), *([cache_c] * pages), *([cache_p] * pages))


def _uv_body(o_ref, wuv_ref, out_ref):
    for hh in range(MLA_HEADS):
        out_ref[:, hh * V_HEAD_DIM:(hh + 1) * V_HEAD_DIM] = _dot(
            o_ref[hh].astype(BF16), wuv_ref[hh]).astype(BF16)


def _uv(o_lat, w_uvh):
    _, m, _ = o_lat.shape
    return pl.pallas_call(
        _uv_body,
        out_shape=jax.ShapeDtypeStruct((m, MLA_HEADS * V_HEAD_DIM), BF16),
        name="uv",
    )(o_lat, w_uvh)


def _oproj_body(md_ref, mm_ref, wo_ref, x_ref, g_ref, o_ref):
    half = md_ref.shape[1]
    y = _dot(md_ref[...], wo_ref[0:half, :]) + _dot(mm_ref[...], wo_ref[half:2 * half, :])
    o_ref[...] = x_ref[...] + _rms(y, g_ref[...])


def _oproj(mixed_d, mixed_m, w_o, x, gain, *, tm):
    m, d = x.shape
    half = mixed_d.shape[1]
    row = lambda i: (i, 0)
    const = lambda i: (0, 0)
    return pl.pallas_call(
        _oproj_body,
        out_shape=jax.ShapeDtypeStruct((m, d), F32),
        grid=(m // tm,),
        in_specs=[pl.BlockSpec((tm, half), row),
                  pl.BlockSpec((tm, half), row),
                  pl.BlockSpec(w_o.shape, const),
                  pl.BlockSpec((tm, d), row),
                  pl.BlockSpec((1, d), const)],
        out_specs=pl.BlockSpec((tm, d), row),
        compiler_params=_params(("parallel",)),
        name="oproj",
    )(mixed_d, mixed_m, w_o, x, gain)


def _rope_tables(pos):
    half = DIFF_HEAD_DIM // 2
    inv = jnp.exp(-math.log(ROPE_THETA) * jnp.arange(half, dtype=F32) * (2.0 / DIFF_HEAD_DIM))
    ang = pos[:, None] * inv[None, :]
    cos, sin = jnp.cos(ang), jnp.sin(ang)
    return jnp.tile(cos, (1, 4)), jnp.concatenate([-sin, sin, -sin, sin], axis=1)


def kernel(x_prompt, x_sample, cache_diff_k, cache_diff_v, cache_mla_ckv, cache_mla_kpe, page_table, ln_ffn1_pre, ln_ffn1_post, ffn1_w_gate, ffn1_w_up, ffn1_w_down, ln_mix_pre, ln_mix_post, w_in, diff_lambda_q1, diff_lambda_k1, diff_lambda_q2, diff_lambda_k2, diff_subln, mla_q_norm, w_uq, mla_kv_norm, w_uk, w_uv, w_o, ln_ffn2_pre, ln_ffn2_post, ffn2_w_gate, ffn2_w_up, ffn2_w_down):
    bsz, t_p, d = x_prompt.shape
    nb, t_s, _ = x_sample.shape
    assert t_s == 1, "one new token per sample"
    depth = w_in.shape[0]
    n_phys, page = cache_diff_k.shape[1:3]
    past_len = page_table.shape[1] * page

    xp = x_prompt.reshape(bsz * t_p, d)
    xs = x_sample.reshape(nb * t_s, d)
    cos_p, sin_p = _rope_tables(jnp.tile(jnp.arange(t_p, dtype=F32), bsz))
    cos_s, sin_s = _rope_tables(jnp.full((nb,), past_len, dtype=F32))

    new_p = [[], [], [], []]
    new_s = [[], [], [], []]
    for l in range(depth):
        lam_init = 0.8 - 0.6 * math.exp(-0.3 * l)
        bf = lambda w: w.astype(BF16)
        ffn1 = (ln_ffn1_pre[l][None], ln_ffn1_post[l][None], bf(ffn1_w_gate[l]), bf(ffn1_w_up[l]), bf(ffn1_w_down[l]))
        ffn2 = (ln_ffn2_pre[l][None], ln_ffn2_post[l][None], bf(ffn2_w_gate[l]), bf(ffn2_w_up[l]), bf(ffn2_w_down[l]))
        w_in_p = bf(jnp.pad(w_in[l], ((0, 0), (0, LANES - QK_ROPE_DIM))))
        w_uq_h = w_uq[l].reshape(Q_LORA_RANK, MLA_HEADS, QK_NOPE_DIM + QK_ROPE_DIM)
        w_uq_p = bf(jnp.concatenate([
            w_uq_h[:, :, :QK_NOPE_DIM].reshape(Q_LORA_RANK, MLA_HEADS * QK_NOPE_DIM),
            jnp.pad(w_uq_h[:, :, QK_NOPE_DIM:], ((0, 0), (0, 0), (0, LANES - QK_ROPE_DIM))
                    ).reshape(Q_LORA_RANK, MLA_HEADS * LANES)], axis=1))
        w_ukt = bf(jnp.transpose(w_uk[l], (1, 2, 0)))
        w_uvh = bf(jnp.transpose(w_uv[l], (1, 0, 2)))
        w_o_b = bf(w_o[l])
        proj_w = (ln_mix_pre[l][None], w_in_p, mla_q_norm[l][None], w_uq_p, mla_kv_norm[l][None], w_ukt)
        lams = (diff_lambda_q1[l][None], diff_lambda_k1[l][None], diff_lambda_q2[l][None], diff_lambda_k2[l][None])
        subln = diff_subln[l][None]

        xp = _ffn(xp, *ffn1, tm=512, tf=512)
        xs = _ffn(xs, *ffn1, tm=nb, tf=512)

        qd, kdf, kdb, vdf, vdb, ckv, kpe, kvm, qm = _proj(xp, *proj_w, cos_p, sin_p, tm=256)
        mixed_d = _diff_flash(qd.reshape(bsz, t_p, -1), kdb.reshape(bsz, t_p, -1), vdb.reshape(bsz, t_p, -1),
                              *lams, subln, tq=256, lam_init=lam_init)
        mixed_m = _mla_flash(qm, kvm.reshape(bsz, t_p, -1), w_uvh, tq=128, tk=512)
        xp = _oproj(mixed_d.reshape(bsz * t_p, -1), mixed_m.reshape(bsz * t_p, -1), w_o_b, xp,
                    ln_mix_post[l][None], tm=256)
        for acc, val in zip(new_p, (kdf, vdf, ckv, kpe)):
            acc.append(val)

        qd, kdf, kdb, vdf, vdb, ckv, kpe, kvm, qm = _proj(xs, *proj_w, cos_s, sin_s, tm=nb)
        q5 = qd.astype(F32).reshape(nb, DIFF_KV_HEADS, DIFF_GROUP, 2, DIFF_HEAD_DIM)
        qbd = jnp.einsum('bkgcd,kK,cC->bckgKCd', q5, jnp.eye(DIFF_KV_HEADS, dtype=F32), jnp.eye(2, dtype=F32))
        qbd = qbd.reshape(nb, 2 * DIFF_HEADS, DIFF_K_COLS).astype(BF16)
        o_d, o_lat = _decode(
            page_table, qbd, jnp.transpose(qm, (1, 0, 2)),
            kdf[:, None], vdf[:, None], ckv[:, None], kpe[:, None], *lams, subln,
            cache_diff_k[l].reshape(n_phys, page, DIFF_K_COLS), cache_diff_v[l].reshape(n_phys, page, DIFF_V_COLS),
            cache_mla_ckv[l], cache_mla_kpe[l], pages=8, lam_init=lam_init)
        mixed_d = o_d.reshape(nb, DIFF_HEADS * DIFF_V_DIM).astype(BF16)
        mixed_m = _uv(jnp.transpose(o_lat, (1, 0, 2)), w_uvh)
        xs = _oproj(mixed_d, mixed_m, w_o_b, xs, ln_mix_post[l][None], tm=nb)
        for acc, val in zip(new_s, (kdf, vdf, ckv, kpe)):
            acc.append(val)

        xp = _ffn(xp, *ffn2, tm=512, tf=512)
        xs = _ffn(xs, *ffn2, tm=nb, tf=512)

    kd_shape = (DIFF_KV_HEADS, 2, DIFF_HEAD_DIM)
    vd_shape = (DIFF_KV_HEADS, DIFF_V_DIM)
    stack = lambda vals, lead, tail: jnp.stack([v.reshape(*lead, *tail) for v in vals])
    lead_p, lead_s = (bsz, t_p), (nb, t_s)
    return (xp.reshape(bsz, t_p, d), xs.reshape(nb, t_s, d),
            stack(new_p[0], lead_p, kd_shape), stack(new_p[1], lead_p, vd_shape),
            stack(new_p[2], lead_p, (KV_LORA_RANK,)), stack(new_p[3], lead_p, (QK_ROPE_DIM,)),
            stack(new_s[0], lead_s, kd_shape), stack(new_s[1], lead_s, vd_shape),
            stack(new_s[2], lead_s, (KV_LORA_RANK,)), stack(new_s[3], lead_s, (QK_ROPE_DIM,)))
```

```python
import functools
import math

import jax
import jax.numpy as jnp
from jax import lax
from jax.experimental import pallas as pl
from jax.experimental.pallas import tpu as pltpu

F32 = jnp.float32
BF16 = jnp.bfloat16

EPS = 1e-6
ROPE_THETA = 10000.0
MASK_VALUE = -1e30
LOG2E = math.log2(math.e)

DIFF_HEADS = 8
DIFF_KV_HEADS = 4
DIFF_GROUP = DIFF_HEADS // DIFF_KV_HEADS
DIFF_HEAD_DIM = 64
DIFF_V_DIM = 2 * DIFF_HEAD_DIM
MLA_HEADS = 8
QK_NOPE_DIM = 128
QK_ROPE_DIM = 64
V_HEAD_DIM = 128
Q_LORA_RANK = 512
KV_LORA_RANK = 256

DIFF_Q_COLS = DIFF_HEADS * 2 * DIFF_HEAD_DIM
DIFF_K_COLS = DIFF_KV_HEADS * 2 * DIFF_HEAD_DIM
DIFF_V_COLS = DIFF_KV_HEADS * DIFF_V_DIM
MLA_KEY_COLS = KV_LORA_RANK + 2 * QK_ROPE_DIM

LANES = 128
V7X_VMEM_LIMIT = 56 * 1024 * 1024

FFN_ROWS, FFN_HIDDEN = 512, 512
PROJ_ROWS = 256
OPROJ_ROWS = 256
DIFF_Q_TILE, MLA_Q_TILE = 256, 128
KEY_TILE = 2 * PROJ_ROWS
DECODE_PAGES = 16

_NT = (((1,), (1,)), ((), ()))


def _dot(a, b):
    return jnp.dot(a, b, preferred_element_type=F32)


def _dot_nt(a, b):
    return lax.dot_general(a, b, _NT, preferred_element_type=F32)


def _rms(x, g, axis=-1):
    return x * lax.rsqrt(jnp.mean(x * x, axis=axis, keepdims=True) + EPS) * g


def _params(sem, vmem=V7X_VMEM_LIMIT):
    return pltpu.CompilerParams(dimension_semantics=sem, vmem_limit_bytes=vmem)


def _ffn_body(x_ref, pre_ref, post_ref, wg_ref, wu_ref, wd_ref, o_ref, h_ref, acc_ref, *, nf):
    f = pl.program_id(1)

    @pl.when(f == 0)
    def _():
        h_ref[...] = _rms(x_ref[...], pre_ref[...]).astype(BF16)
        acc_ref[...] = jnp.zeros_like(acc_ref)

    h = h_ref[...]
    g = _dot(h, wg_ref[...])
    u = _dot(h, wu_ref[...])
    a = (g / (1.0 + jnp.exp(-g))) * u
    acc_ref[...] += _dot(a.astype(BF16), wd_ref[...])

    @pl.when(f == nf - 1)
    def _():
        o_ref[...] = x_ref[...] + 0.5 * _rms(acc_ref[...], post_ref[...])


def _ffn(x, pre, post, wg, wu, wd, *, tm, tf):
    m, d = x.shape
    nf = wg.shape[1] // tf
    row = lambda i, f: (i, 0)
    const = lambda i, f: (0, 0)
    return pl.pallas_call(
        functools.partial(_ffn_body, nf=nf),
        out_shape=jax.ShapeDtypeStruct((m, d), F32),
        grid=(m // tm, nf),
        in_specs=[pl.BlockSpec((tm, d), row),
                  pl.BlockSpec((1, d), const),
                  pl.BlockSpec((1, d), const),
                  pl.BlockSpec((d, tf), lambda i, f: (0, f)),
                  pl.BlockSpec((d, tf), lambda i, f: (0, f)),
                  pl.BlockSpec((tf, d), lambda i, f: (f, 0))],
        out_specs=pl.BlockSpec((tm, d), row),
        scratch_shapes=[pltpu.VMEM((tm, d), BF16), pltpu.VMEM((tm, d), F32)],
        compiler_params=_params(("parallel", "arbitrary")),
        name="ffn",
    )(x, pre, post, wg, wu, wd)


def _proj_body(x_ref, g_ref, win_ref, qn_ref, wuq_ref, kvn_ref, wukt_ref, cos_ref, sin_ref,
               qd_ref, kdf_ref, kdb_ref, vdf_ref, vdt_ref, ckv_ref, ckvt_ref, kpe_ref, kvm_ref, qm_ref):
    h = _rms(x_ref[...], g_ref[...]).astype(BF16)
    cos = cos_ref[...]
    sin = sin_ref[...]
    lane = lax.broadcasted_iota(jnp.int32, cos.shape, 1)
    first_half = (lane & (DIFF_HEAD_DIM - 1)) < DIFF_HEAD_DIM // 2

    def rope(z):
        swapped = jnp.where(first_half, pltpu.roll(z, LANES - 32, 1), pltpu.roll(z, 32, 1))
        return z * cos + swapped * sin

    c0 = 0
    zq = _dot(h, win_ref[:, c0:c0 + DIFF_Q_COLS])
    diff_scale = DIFF_HEAD_DIM ** -0.5 * LOG2E
    for j in range(DIFF_Q_COLS // LANES):
        sl = slice(j * LANES, (j + 1) * LANES)
        qd_ref[:, sl] = (rope(zq[:, sl]) * diff_scale).astype(BF16)
    c0 += DIFF_Q_COLS

    zk = _dot(h, win_ref[:, c0:c0 + DIFF_K_COLS])
    for j in range(DIFF_K_COLS // LANES):
        sl = slice(j * LANES, (j + 1) * LANES)
        r = rope(zk[:, sl])
        kdf_ref[:, sl] = r
        kdb_ref[:, sl] = r.astype(BF16)
    c0 += DIFF_K_COLS

    zv = _dot(h, win_ref[:, c0:c0 + DIFF_V_COLS])
    vdf_ref[...] = zv
    vdt_ref[0] = zv.T.astype(BF16)
    c0 += DIFF_V_COLS

    cq = _rms(_dot(h, win_ref[:, c0:c0 + Q_LORA_RANK]), qn_ref[...]).astype(BF16)
    c0 += Q_LORA_RANK
    qm = _dot(cq, wuq_ref[...])
    mla_scale = (QK_NOPE_DIM + QK_ROPE_DIM) ** -0.5 * LOG2E
    pe0 = MLA_HEADS * QK_NOPE_DIM
    for hh in range(MLA_HEADS):
        q_nope = qm[:, hh * QK_NOPE_DIM:(hh + 1) * QK_NOPE_DIM].astype(BF16)
        q_lat = _dot(q_nope, wukt_ref[hh])
        qm_ref[hh, :, 0:KV_LORA_RANK] = (q_lat * mla_scale).astype(BF16)
        q_pe = rope(qm[:, pe0 + hh * LANES:pe0 + (hh + 1) * LANES])
        qm_ref[hh, :, KV_LORA_RANK:MLA_KEY_COLS] = (q_pe * mla_scale).astype(BF16)

    ckv = _rms(_dot(h, win_ref[:, c0:c0 + KV_LORA_RANK]), kvn_ref[...])
    ckv_ref[...] = ckv
    kvm_ref[:, 0:KV_LORA_RANK] = ckv.astype(BF16)
    ckvt_ref[0] = ckv.T.astype(BF16)
    c0 += KV_LORA_RANK

    kpe = rope(_dot(h, win_ref[:, c0:c0 + LANES]))
    kpe_ref[...] = kpe[:, 0:QK_ROPE_DIM]
    kvm_ref[:, KV_LORA_RANK:MLA_KEY_COLS] = kpe.astype(BF16)


def _proj(x, gain, w_in, q_norm, w_uq, kv_norm, w_ukt, cos, sin, *, tm):
    m, d = x.shape
    row = lambda i: (i, 0)
    chunk = lambda i: (i, 0, 0)
    const2 = lambda i: (0, 0)
    const3 = lambda i: (0, 0, 0)
    out_shape = (
        jax.ShapeDtypeStruct((m, DIFF_Q_COLS), BF16),
        jax.ShapeDtypeStruct((m, DIFF_K_COLS), F32),
        jax.ShapeDtypeStruct((m, DIFF_K_COLS), BF16),
        jax.ShapeDtypeStruct((m, DIFF_V_COLS), F32),
        jax.ShapeDtypeStruct((m // tm, DIFF_V_COLS, tm), BF16),
        jax.ShapeDtypeStruct((m, KV_LORA_RANK), F32),
        jax.ShapeDtypeStruct((m // tm, KV_LORA_RANK, tm), BF16),
        jax.ShapeDtypeStruct((m, QK_ROPE_DIM), F32),
        jax.ShapeDtypeStruct((m, MLA_KEY_COLS), BF16),
        jax.ShapeDtypeStruct((MLA_HEADS, m, MLA_KEY_COLS), BF16),
    )
    out_specs = (
        pl.BlockSpec((tm, DIFF_Q_COLS), row),
        pl.BlockSpec((tm, DIFF_K_COLS), row),
        pl.BlockSpec((tm, DIFF_K_COLS), row),
        pl.BlockSpec((tm, DIFF_V_COLS), row),
        pl.BlockSpec((1, DIFF_V_COLS, tm), chunk),
        pl.BlockSpec((tm, KV_LORA_RANK), row),
        pl.BlockSpec((1, KV_LORA_RANK, tm), chunk),
        pl.BlockSpec((tm, QK_ROPE_DIM), row),
        pl.BlockSpec((tm, MLA_KEY_COLS), row),
        pl.BlockSpec((MLA_HEADS, tm, MLA_KEY_COLS), lambda i: (0, i, 0)),
    )
    return pl.pallas_call(
        _proj_body,
        out_shape=out_shape,
        grid=(m // tm,),
        in_specs=[pl.BlockSpec((tm, d), row),
                  pl.BlockSpec((1, d), const2),
                  pl.BlockSpec(w_in.shape, const2),
                  pl.BlockSpec((1, Q_LORA_RANK), const2),
                  pl.BlockSpec(w_uq.shape, const2),
                  pl.BlockSpec((1, KV_LORA_RANK), const2),
                  pl.BlockSpec(w_ukt.shape, const3),
                  pl.BlockSpec((tm, LANES), row),
                  pl.BlockSpec((tm, LANES), row)],
        out_specs=out_specs,
        compiler_params=_params(("parallel",)),
        name="proj",
    )(x, gain, w_in, q_norm, w_uq, kv_norm, w_ukt, cos, sin)


def _diff_lambda(lq1, lk1, lq2, lk2, lam_init):
    a = jnp.exp(jnp.sum(lq1[...] * lk1[...], axis=-1, keepdims=True))
    b = jnp.exp(jnp.sum(lq2[...] * lk2[...], axis=-1, keepdims=True))
    return a - b + lam_init


def _online_update(s, m_ref, l_ref):
    m_prev = m_ref[...]
    m_new = jnp.maximum(m_prev, jnp.max(s, axis=-1, keepdims=True))
    alpha = jnp.exp2(m_prev - m_new)
    p = jnp.exp2(s - m_new)
    l_ref[...] = alpha * l_ref[...] + jnp.sum(p, axis=-1, keepdims=True)
    m_ref[...] = m_new
    return alpha, p


def _init_softmax_state(m_ref, l_ref, acc_ref):
    m_ref[...] = jnp.full_like(m_ref, MASK_VALUE)
    l_ref[...] = jnp.zeros_like(l_ref)
    acc_ref[...] = jnp.zeros_like(acc_ref)


ONES_ROWS = 16


def _flash_update_t(s, vt, ones, m_ref, acc_ref):
    m_prev = m_ref[...]
    m_new = jnp.maximum(m_prev, jnp.max(s, axis=0, keepdims=True))
    alpha = jnp.exp2(m_prev - m_new)
    p = jnp.exp2(s - m_new).astype(BF16)
    acc_ref[...] = alpha * acc_ref[...] + _dot(jnp.concatenate([vt, ones], axis=0), p)
    m_ref[...] = m_new


def _init_flash_state_t(m_ref, acc_ref):
    m_ref[...] = jnp.full_like(m_ref, MASK_VALUE)
    acc_ref[...] = jnp.zeros_like(acc_ref)


def _causal_key_loop(step, qi, tq, tk):
    def body(j, carry):
        step(j, False)
        return carry

    n_full = (qi * tq) // tk
    lax.fori_loop(0, n_full, body, 0)
    step(n_full, True)


def _causal_mask_t(shape, key_start, q_start, tq):
    key = lax.broadcasted_iota(jnp.int32, shape, 0) + key_start
    query = (lax.broadcasted_iota(jnp.int32, shape, 1) & (tq - 1)) + q_start
    return key <= query


def _diff_flash_body(q_ref, k_ref, vt_ref, lq1, lk1, lq2, lk2, subln_ref, o_ref,
                     m_ref, acc_ref, *, tq, tk, lam_init):
    qi = pl.program_id(2)
    chunks = tk // vt_ref.shape[2]
    q = q_ref[0]
    lane = lax.broadcasted_iota(jnp.int32, (tq, LANES), 1)
    comp0 = lane < DIFF_HEAD_DIM
    qa, qb = q[:, 0:LANES], q[:, LANES:2 * LANES]
    zero = jnp.zeros_like(qa)
    q4 = jnp.concatenate([jnp.where(comp0, qa, zero), jnp.where(comp0, qb, zero),
                          jnp.where(comp0, zero, qa), jnp.where(comp0, zero, qb)], axis=0)
    ones = jnp.ones((ONES_ROWS, tk), BF16)

    _init_flash_state_t(m_ref, acc_ref)

    def step(j, masked):
        start = pl.multiple_of(j * tk, tk)
        k = k_ref[0, pl.ds(start, tk), :]
        vt = jnp.concatenate([vt_ref[j * chunks + c] for c in range(chunks)], axis=1)
        s = _dot_nt(k, q4)
        if masked:
            s = jnp.where(_causal_mask_t(s.shape, start, qi * tq, tq), s, MASK_VALUE)
        _flash_update_t(s, vt, ones, m_ref, acc_ref)

    _causal_key_loop(step, qi, tq, tk)

    acc = acc_ref[...]
    o = acc[0:DIFF_V_DIM] / acc[DIFF_V_DIM:DIFF_V_DIM + 1]
    lam = _diff_lambda(lq1, lk1, lq2, lk2, lam_init)
    a = o[:, 0:2 * tq] - lam * o[:, 2 * tq:4 * tq]
    y = _rms(a, subln_ref[...], axis=0) * (1.0 - lam_init)
    o_ref[0, :, 0:LANES] = y[:, 0:tq].T.astype(BF16)
    o_ref[0, :, LANES:2 * LANES] = y[:, tq:2 * tq].T.astype(BF16)


def _diff_flash(qd, kd, vdt, lq1, lk1, lq2, lk2, subln_col, *, tq, tk, lam_init):
    b, t, _ = qd.shape
    n_chunk, _, chunk = vdt.shape
    per_batch = n_chunk // b
    const = lambda bi, ki, qi: (0, 0)
    lam_spec = pl.BlockSpec((1, DIFF_HEAD_DIM), const)
    return pl.pallas_call(
        functools.partial(_diff_flash_body, tq=tq, tk=tk, lam_init=lam_init),
        out_shape=jax.ShapeDtypeStruct((b, t, DIFF_HEADS * DIFF_V_DIM), BF16),
        grid=(b, DIFF_KV_HEADS, t // tq),
        in_specs=[pl.BlockSpec((1, tq, 2 * LANES), lambda bi, ki, qi: (bi, qi, ki)),
                  pl.BlockSpec((1, t, LANES), lambda bi, ki, qi: (bi, 0, ki)),
                  pl.BlockSpec((per_batch, DIFF_V_DIM, chunk), lambda bi, ki, qi: (bi, ki, 0)),
                  lam_spec, lam_spec, lam_spec, lam_spec,
                  pl.BlockSpec((DIFF_V_DIM, 1), const)],
        out_specs=pl.BlockSpec((1, tq, 2 * LANES), lambda bi, ki, qi: (bi, qi, ki)),
        scratch_shapes=[pltpu.VMEM((1, 4 * tq), F32),
                        pltpu.VMEM((DIFF_V_DIM + ONES_ROWS, 4 * tq), F32)],
        compiler_params=_params(("parallel", "parallel", "arbitrary")),
        name="diff_flash",
    )(qd, kd, vdt, lq1, lk1, lq2, lk2, subln_col)


def _mla_flash_body(q_ref, kv_ref, ckvt_ref, wuvt_ref, o_ref, m_ref, acc_ref, *, tq, tk):
    qi = pl.program_id(1)
    chunks = tk // ckvt_ref.shape[2]
    q = jnp.concatenate([q_ref[hh] for hh in range(MLA_HEADS)], axis=0)
    ones = jnp.ones((ONES_ROWS, tk), BF16)

    _init_flash_state_t(m_ref, acc_ref)

    def step(j, masked):
        start = pl.multiple_of(j * tk, tk)
        kv = kv_ref[0, pl.ds(start, tk), :]
        ct = jnp.concatenate([ckvt_ref[j * chunks + c] for c in range(chunks)], axis=1)
        s = _dot_nt(kv, q)
        if masked:
            s = jnp.where(_causal_mask_t(s.shape, start, qi * tq, tq), s, MASK_VALUE)
        _flash_update_t(s, ct, ones, m_ref, acc_ref)

    _causal_key_loop(step, qi, tq, tk)

    acc = acc_ref[...]
    o = (acc[0:KV_LORA_RANK] / acc[KV_LORA_RANK:KV_LORA_RANK + 1]).astype(BF16)
    for hh in range(MLA_HEADS):
        out_t = _dot(wuvt_ref[hh], o[:, hh * tq:(hh + 1) * tq])
        o_ref[0, :, hh * V_HEAD_DIM:(hh + 1) * V_HEAD_DIM] = out_t.T.astype(BF16)


def _mla_flash(qm, kvm, ckvt, w_uvt, *, tq, tk):
    b, t, _ = kvm.shape
    n_chunk, _, chunk = ckvt.shape
    per_batch = n_chunk // b
    nq = t // tq
    return pl.pallas_call(
        functools.partial(_mla_flash_body, tq=tq, tk=tk),
        out_shape=jax.ShapeDtypeStruct((b, t, MLA_HEADS * V_HEAD_DIM), BF16),
        grid=(b, nq),
        in_specs=[pl.BlockSpec((MLA_HEADS, tq, MLA_KEY_COLS), lambda bi, qi: (0, bi * nq + qi, 0)),
                  pl.BlockSpec((1, t, MLA_KEY_COLS), lambda bi, qi: (bi, 0, 0)),
                  pl.BlockSpec((per_batch, KV_LORA_RANK, chunk), lambda bi, qi: (bi, 0, 0)),
                  pl.BlockSpec(w_uvt.shape, lambda bi, qi: (0, 0, 0))],
        out_specs=pl.BlockSpec((1, tq, MLA_HEADS * V_HEAD_DIM), lambda bi, qi: (bi, qi, 0)),
        scratch_shapes=[pltpu.VMEM((1, MLA_HEADS * tq), F32),
                        pltpu.VMEM((KV_LORA_RANK + ONES_ROWS, MLA_HEADS * tq), F32)],
        compiler_params=_params(("parallel", "arbitrary")),
        name="mla_flash",
    )(qm, kvm, ckvt, w_uvt)


def _decode_body(pt_ref, qbd_ref, qm_ref, knew_ref, vnew_ref, cnew_ref, pnew_ref,
                 lq1, lk1, lq2, lk2, subln_ref, *rest, pages, lam_init):
    del pt_ref
    k_refs = rest[0:pages]
    v_refs = rest[pages:2 * pages]
    c_refs = rest[2 * pages:3 * pages]
    p_refs = rest[3 * pages:4 * pages]
    od_ref, om_ref, md_ref, ld_ref, accd_ref, mm_ref, lm_ref, accm_ref = rest[4 * pages:]
    step = pl.program_id(1)
    n_rows_d = 2 * DIFF_HEADS
    page = c_refs[0].shape[0]

    @pl.when(step == 0)
    def _():
        _init_softmax_state(md_ref, ld_ref, accd_ref)
        _init_softmax_state(mm_ref, lm_ref, accm_ref)

    qbd = qbd_ref[0]
    qm = qm_ref[0]
    q_lat = qm[:, 0:KV_LORA_RANK]
    q_pe = qm[:, KV_LORA_RANK:KV_LORA_RANK + QK_ROPE_DIM]
    row_head = (lax.broadcasted_iota(jnp.int32, (n_rows_d, DIFF_V_DIM), 0) & (DIFF_HEADS - 1)) // DIFF_GROUP

    sd = jnp.concatenate([_dot(qbd, k_refs[i][...].astype(BF16)) for i in range(pages)], axis=1)
    alpha, p = _online_update(sd, md_ref, ld_ref)
    p = p.astype(BF16)
    pv = jnp.zeros((n_rows_d, DIFF_V_DIM), F32)
    for kk in range(DIFF_KV_HEADS):
        pv_k = _dot(p[:, 0:page], v_refs[0][pl.ds(kk, page, stride=DIFF_KV_HEADS), :].astype(BF16))
        for i in range(1, pages):
            pv_k += _dot(p[:, i * page:(i + 1) * page],
                         v_refs[i][pl.ds(kk, page, stride=DIFF_KV_HEADS), :].astype(BF16))
        pv += jnp.where(row_head == kk, pv_k, 0.0)
    accd_ref[...] = alpha * accd_ref[...] + pv

    cs = [c_refs[i][...].astype(BF16) for i in range(pages)]
    sm = jnp.concatenate([_dot_nt(q_lat, cs[i]) + _dot(q_pe, p_refs[i][...].astype(BF16))
                          for i in range(pages)], axis=1)
    alpha, p = _online_update(sm, mm_ref, lm_ref)
    p = p.astype(BF16)
    pv = _dot(p[:, 0:page], cs[0])
    for i in range(1, pages):
        pv += _dot(p[:, i * page:(i + 1) * page], cs[i])
    accm_ref[...] = alpha * accm_ref[...] + pv

    @pl.when(step == pl.num_programs(1) - 1)
    def _():
        round_bf16 = lambda a: a.astype(BF16).astype(F32)
        k_new = round_bf16(knew_ref[0])
        v_new = round_bf16(vnew_ref[0])
        s_new = jnp.sum(qbd.astype(F32) * k_new, axis=-1, keepdims=True)
        alpha, p_new = _online_update(s_new, md_ref, ld_ref)
        v_own = jnp.zeros((n_rows_d, DIFF_V_DIM), F32)
        for kk in range(DIFF_KV_HEADS):
            v_own += jnp.where(row_head == kk, v_new[:, kk * DIFF_V_DIM:(kk + 1) * DIFF_V_DIM], 0.0)
        o = (alpha * accd_ref[...] + round_bf16(p_new) * v_own) / ld_ref[...]
        lam = _diff_lambda(lq1, lk1, lq2, lk2, lam_init)
        a = o[0:DIFF_HEADS] - lam * o[DIFF_HEADS:n_rows_d]
        od_ref[0] = _rms(a, subln_ref[...]) * (1.0 - lam_init)

        c_new = round_bf16(cnew_ref[0])
        pe_new = round_bf16(pnew_ref[0])
        s_new = (jnp.sum(q_lat.astype(F32) * c_new, axis=-1, keepdims=True)
                 + jnp.sum(q_pe.astype(F32) * pe_new, axis=-1, keepdims=True))
        alpha, p_new = _online_update(s_new, mm_ref, lm_ref)
        om_ref[0] = (alpha * accm_ref[...] + round_bf16(p_new) * c_new) / lm_ref[...]


def _decode(page_table, qbd, qm, k_new, v_new, c_new, pe_new, lq1, lk1, lq2, lk2, subln,
            cache_kt, cache_v, cache_c, cache_pt, *, pages, lam_init):
    nb, n_pages = page_table.shape
    per_sample = lambda b, s, pt: (b, 0, 0)
    const = lambda b, s, pt: (0, 0)
    lam_spec = pl.BlockSpec((1, DIFF_HEAD_DIM), const)

    def paged(cache):
        return [pl.BlockSpec((None,) + cache.shape[1:],
                             functools.partial(lambda b, s, pt, i: (pt[b, s * pages + i], 0, 0), i=i))
                for i in range(pages)]

    n_rows_d = 2 * DIFF_HEADS
    grid_spec = pltpu.PrefetchScalarGridSpec(
        num_scalar_prefetch=1,
        grid=(nb, n_pages // pages),
        in_specs=[pl.BlockSpec((1, n_rows_d, DIFF_K_COLS), per_sample),
                  pl.BlockSpec((1, MLA_HEADS, MLA_KEY_COLS), per_sample),
                  pl.BlockSpec((1, 1, DIFF_K_COLS), per_sample),
                  pl.BlockSpec((1, 1, DIFF_V_COLS), per_sample),
                  pl.BlockSpec((1, 1, KV_LORA_RANK), per_sample),
                  pl.BlockSpec((1, 1, QK_ROPE_DIM), per_sample),
                  lam_spec, lam_spec, lam_spec, lam_spec,
                  pl.BlockSpec((1, DIFF_V_DIM), const)]
                 + paged(cache_kt) + paged(cache_v) + paged(cache_c) + paged(cache_pt),
        out_specs=(pl.BlockSpec((1, DIFF_HEADS, DIFF_V_DIM), per_sample),
                   pl.BlockSpec((1, MLA_HEADS, KV_LORA_RANK), per_sample)),
        scratch_shapes=[pltpu.VMEM((n_rows_d, 1), F32), pltpu.VMEM((n_rows_d, 1), F32),
                        pltpu.VMEM((n_rows_d, DIFF_V_DIM), F32),
                        pltpu.VMEM((MLA_HEADS, 1), F32), pltpu.VMEM((MLA_HEADS, 1), F32),
                        pltpu.VMEM((MLA_HEADS, KV_LORA_RANK), F32)],
    )
    return pl.pallas_call(
        functools.partial(_decode_body, pages=pages, lam_init=lam_init),
        out_shape=(jax.ShapeDtypeStruct((nb, DIFF_HEADS, DIFF_V_DIM), F32),
                   jax.ShapeDtypeStruct((nb, MLA_HEADS, KV_LORA_RANK), F32)),
        grid_spec=grid_spec,
        compiler_params=_params(("parallel", "arbitrary")),
        name="decode",
    )(page_table, qbd, qm, k_new, v_new, c_new, pe_new, lq1, lk1, lq2, lk2, subln,
      *([cache_kt] * pages), *([cache_v] * pages), *([cache_c] * pages), *([cache_pt] * pages))


def _uv_body(o_ref, wuv_ref, out_ref):
    for hh in range(MLA_HEADS):
        out_ref[:, hh * V_HEAD_DIM:(hh + 1) * V_HEAD_DIM] = _dot(
            o_ref[hh].astype(BF16), wuv_ref[hh]).astype(BF16)


def _uv(o_lat, w_uvh):
    _, m, _ = o_lat.shape
    return pl.pallas_call(
        _uv_body,
        out_shape=jax.ShapeDtypeStruct((m, MLA_HEADS * V_HEAD_DIM), BF16),
        name="uv",
    )(o_lat, w_uvh)


def _oproj_body(md_ref, mm_ref, wo_ref, x_ref, g_ref, o_ref):
    half = md_ref.shape[1]
    y = _dot(md_ref[...], wo_ref[0:half, :]) + _dot(mm_ref[...], wo_ref[half:2 * half, :])
    o_ref[...] = x_ref[...] + _rms(y, g_ref[...])


def _oproj(mixed_d, mixed_m, w_o, x, gain, *, tm):
    m, d = x.shape
    half = mixed_d.shape[1]
    row = lambda i: (i, 0)
    const = lambda i: (0, 0)
    return pl.pallas_call(
        _oproj_body,
        out_shape=jax.ShapeDtypeStruct((m, d), F32),
        grid=(m // tm,),
        in_specs=[pl.BlockSpec((tm, half), row),
                  pl.BlockSpec((tm, half), row),
                  pl.BlockSpec(w_o.shape, const),
                  pl.BlockSpec((tm, d), row),
                  pl.BlockSpec((1, d), const)],
        out_specs=pl.BlockSpec((tm, d), row),
        compiler_params=_params(("parallel",)),
        name="oproj",
    )(mixed_d, mixed_m, w_o, x, gain)


def _rope_tables(pos):
    half = DIFF_HEAD_DIM // 2
    inv = jnp.exp(-math.log(ROPE_THETA) * jnp.arange(half, dtype=F32) * (2.0 / DIFF_HEAD_DIM))
    ang = pos[:, None] * inv[None, :]
    cos, sin = jnp.cos(ang), jnp.sin(ang)
    return jnp.tile(cos, (1, 4)), jnp.concatenate([-sin, sin, -sin, sin], axis=1)


def kernel(x_prompt, x_sample, cache_diff_k, cache_diff_v, cache_mla_ckv, cache_mla_kpe, page_table, ln_ffn1_pre, ln_ffn1_post, ffn1_w_gate, ffn1_w_up, ffn1_w_down, ln_mix_pre, ln_mix_post, w_in, diff_lambda_q1, diff_lambda_k1, diff_lambda_q2, diff_lambda_k2, diff_subln, mla_q_norm, w_uq, mla_kv_norm, w_uk, w_uv, w_o, ln_ffn2_pre, ln_ffn2_post, ffn2_w_gate, ffn2_w_up, ffn2_w_down):
    bsz, t_p, d = x_prompt.shape
    nb, t_s, _ = x_sample.shape
    assert t_s == 1, "one new token per sample"
    depth = w_in.shape[0]
    n_phys, page = cache_diff_k.shape[1:3]
    past_len = page_table.shape[1] * page

    xp = x_prompt.reshape(bsz * t_p, d)
    xs = x_sample.reshape(nb * t_s, d)
    cos_p, sin_p = _rope_tables(jnp.tile(jnp.arange(t_p, dtype=F32), bsz))
    cos_s, sin_s = _rope_tables(jnp.full((nb,), past_len, dtype=F32))

    new_p = [[], [], [], []]
    new_s = [[], [], [], []]
    for l in range(depth):
        lam_init = 0.8 - 0.6 * math.exp(-0.3 * l)
        bf = lambda w: w.astype(BF16)
        ffn1 = (ln_ffn1_pre[l][None], ln_ffn1_post[l][None], bf(ffn1_w_gate[l]), bf(ffn1_w_up[l]), bf(ffn1_w_down[l]))
        ffn2 = (ln_ffn2_pre[l][None], ln_ffn2_post[l][None], bf(ffn2_w_gate[l]), bf(ffn2_w_up[l]), bf(ffn2_w_down[l]))
        w_in_p = bf(jnp.pad(w_in[l], ((0, 0), (0, LANES - QK_ROPE_DIM))))
        w_uq_h = w_uq[l].reshape(Q_LORA_RANK, MLA_HEADS, QK_NOPE_DIM + QK_ROPE_DIM)
        w_uq_p = bf(jnp.concatenate([
            w_uq_h[:, :, :QK_NOPE_DIM].reshape(Q_LORA_RANK, MLA_HEADS * QK_NOPE_DIM),
            jnp.pad(w_uq_h[:, :, QK_NOPE_DIM:], ((0, 0), (0, 0), (0, LANES - QK_ROPE_DIM))
                    ).reshape(Q_LORA_RANK, MLA_HEADS * LANES)], axis=1))
        w_ukt = bf(jnp.transpose(w_uk[l], (1, 2, 0)))
        w_uvh = bf(jnp.transpose(w_uv[l], (1, 0, 2)))
        w_uvt = bf(jnp.transpose(w_uv[l], (1, 2, 0)))
        w_o_b = bf(w_o[l])
        proj_w = (ln_mix_pre[l][None], w_in_p, mla_q_norm[l][None], w_uq_p, mla_kv_norm[l][None], w_ukt)
        lams = (diff_lambda_q1[l][None], diff_lambda_k1[l][None], diff_lambda_q2[l][None], diff_lambda_k2[l][None])
        subln = diff_subln[l][None]

        xp = _ffn(xp, *ffn1, tm=FFN_ROWS, tf=FFN_HIDDEN)
        xs = _ffn(xs, *ffn1, tm=nb, tf=FFN_HIDDEN)

        qd, kdf, kdb, vdf, vdt, ckv, ckvt, kpe, kvm, qm = _proj(xp, *proj_w, cos_p, sin_p, tm=PROJ_ROWS)
        mixed_d = _diff_flash(qd.reshape(bsz, t_p, -1), kdb.reshape(bsz, t_p, -1), vdt, *lams,
                              diff_subln[l][:, None], tq=DIFF_Q_TILE, tk=KEY_TILE, lam_init=lam_init)
        mixed_m = _mla_flash(qm, kvm.reshape(bsz, t_p, -1), ckvt, w_uvt, tq=MLA_Q_TILE, tk=KEY_TILE)
        xp = _oproj(mixed_d.reshape(bsz * t_p, -1), mixed_m.reshape(bsz * t_p, -1), w_o_b, xp,
                    ln_mix_post[l][None], tm=OPROJ_ROWS)
        for acc, val in zip(new_p, (kdf, vdf, ckv, kpe)):
            acc.append(val)

        qd, kdf, _, vdf, _, ckv, _, kpe, _, qm = _proj(xs, *proj_w, cos_s, sin_s, tm=nb)
        q5 = qd.astype(F32).reshape(nb, DIFF_KV_HEADS, DIFF_GROUP, 2, DIFF_HEAD_DIM)
        qbd = jnp.einsum('bkgcd,kK,cC->bckgKCd', q5, jnp.eye(DIFF_KV_HEADS, dtype=F32), jnp.eye(2, dtype=F32))
        qbd = qbd.reshape(nb, 2 * DIFF_HEADS, DIFF_K_COLS).astype(BF16)
        cache_kt = jnp.transpose(cache_diff_k[l].reshape(n_phys, page, DIFF_K_COLS), (0, 2, 1))
        cache_v = cache_diff_v[l].reshape(n_phys, page * DIFF_KV_HEADS, DIFF_V_DIM)
        cache_pt = jnp.transpose(cache_mla_kpe[l], (0, 2, 1))
        o_d, o_lat = _decode(
            page_table, qbd, jnp.transpose(qm, (1, 0, 2)),
            kdf[:, None], vdf[:, None], ckv[:, None], kpe[:, None], *lams, subln,
            cache_kt, cache_v, cache_mla_ckv[l], cache_pt, pages=DECODE_PAGES, lam_init=lam_init)
        mixed_d = o_d.reshape(nb, DIFF_HEADS * DIFF_V_DIM).astype(BF16)
        mixed_m = _uv(jnp.transpose(o_lat, (1, 0, 2)), w_uvh)
        xs = _oproj(mixed_d, mixed_m, w_o_b, xs, ln_mix_post[l][None], tm=nb)
        for acc, val in zip(new_s, (kdf, vdf, ckv, kpe)):
            acc.append(val)

        xp = _ffn(xp, *ffn2, tm=FFN_ROWS, tf=FFN_HIDDEN)
        xs = _ffn(xs, *ffn2, tm=nb, tf=FFN_HIDDEN)

    kd_shape = (DIFF_KV_HEADS, 2, DIFF_HEAD_DIM)
    vd_shape = (DIFF_KV_HEADS, DIFF_V_DIM)
    stack = lambda vals, lead, tail: jnp.stack([v.reshape(*lead, *tail) for v in vals])
    lead_p, lead_s = (bsz, t_p), (nb, t_s)
    return (xp.reshape(bsz, t_p, d), xs.reshape(nb, t_s, d),
            stack(new_p[0], lead_p, kd_shape), stack(new_p[1], lead_p, vd_shape),
            stack(new_p[2], lead_p, (KV_LORA_RANK,)), stack(new_p[3], lead_p, (QK_ROPE_DIM,)),
            stack(new_s[0], lead_s, kd_shape), stack(new_s[1], lead_s, vd_shape),
            stack(new_s[2], lead_s, (KV_LORA_RANK,)), stack(new_s[3], lead_s, (QK_ROPE_DIM,)))
```

```python
import functools
import math

import jax
import jax.numpy as jnp
from jax import lax
from jax.experimental import pallas as pl
from jax.experimental.pallas import tpu as pltpu

F32 = jnp.float32
BF16 = jnp.bfloat16

EPS = 1e-6
ROPE_THETA = 10000.0
MASK_VALUE = -1e30
LOG2E = math.log2(math.e)

DIFF_HEADS = 8
DIFF_KV_HEADS = 4
DIFF_GROUP = DIFF_HEADS // DIFF_KV_HEADS
DIFF_HEAD_DIM = 64
DIFF_V_DIM = 2 * DIFF_HEAD_DIM
MLA_HEADS = 8
QK_NOPE_DIM = 128
QK_ROPE_DIM = 64
V_HEAD_DIM = 128
Q_LORA_RANK = 512
KV_LORA_RANK = 256

DIFF_Q_COLS = DIFF_HEADS * 2 * DIFF_HEAD_DIM
DIFF_K_COLS = DIFF_KV_HEADS * 2 * DIFF_HEAD_DIM
DIFF_V_COLS = DIFF_KV_HEADS * DIFF_V_DIM
MLA_KEY_COLS = KV_LORA_RANK + 2 * QK_ROPE_DIM

LANES = 128
V7X_VMEM_LIMIT = 56 * 1024 * 1024

FFN_ROWS, FFN_HIDDEN = 1024, 256
PROJ_ROWS = 256
OPROJ_ROWS = 256
DIFF_Q_TILE, MLA_Q_TILE = 256, 128
KEY_TILE = 2 * PROJ_ROWS
DECODE_PAGES = 16

_NT = (((1,), (1,)), ((), ()))


def _dot(a, b):
    return jnp.dot(a, b, preferred_element_type=F32)


def _dot_nt(a, b):
    return lax.dot_general(a, b, _NT, preferred_element_type=F32)


def _rms(x, g, axis=-1):
    return x * lax.rsqrt(jnp.mean(x * x, axis=axis, keepdims=True) + EPS) * g


def _params(sem, vmem=V7X_VMEM_LIMIT, flags=None):
    return pltpu.CompilerParams(dimension_semantics=sem, vmem_limit_bytes=vmem, flags=flags)


def _ffn_body(x_ref, pre_ref, post_ref, wg_ref, wu_ref, wd_ref, o_ref, h_ref, *, nf):
    f = pl.program_id(1)

    @pl.when(f == 0)
    def _():
        h_ref[...] = _rms(x_ref[...], pre_ref[...]).astype(BF16)
        o_ref[...] = jnp.zeros_like(o_ref)

    h = h_ref[...]
    g = _dot(h, wg_ref[...].astype(BF16))
    u = _dot(h, wu_ref[...].astype(BF16))
    a = (g / (1.0 + jnp.exp(-g))) * u
    o_ref[...] += _dot(a.astype(BF16), wd_ref[...].astype(BF16))

    @pl.when(f == nf - 1)
    def _():
        o_ref[...] = x_ref[...] + 0.5 * _rms(o_ref[...], post_ref[...])


def _ffn(x, pre, post, wg, wu, wd, *, tm, tf):
    m, d = x.shape
    nf = wg.shape[1] // tf
    row = lambda i, f: (i, 0)
    const = lambda i, f: (0, 0)
    return pl.pallas_call(
        functools.partial(_ffn_body, nf=nf),
        out_shape=jax.ShapeDtypeStruct((m, d), F32),
        grid=(m // tm, nf),
        in_specs=[pl.BlockSpec((tm, d), row, pipeline_mode=pl.Buffered(1)),
                  pl.BlockSpec((1, d), const),
                  pl.BlockSpec((1, d), const),
                  pl.BlockSpec((d, tf), lambda i, f: (0, f)),
                  pl.BlockSpec((d, tf), lambda i, f: (0, f)),
                  pl.BlockSpec((tf, d), lambda i, f: (f, 0))],
        out_specs=pl.BlockSpec((tm, d), row),
        scratch_shapes=[pltpu.VMEM((tm, d), BF16)],
        compiler_params=_params(("parallel", "arbitrary")),
        name="ffn",
    )(x, pre, post, wg, wu, wd)


def _proj_body(x_ref, g_ref, win_ref, qn_ref, wuq_ref, kvn_ref, wukt_ref, cos_ref, sin_ref,
               qd_ref, kdt_ref, kdb_ref, vdf_ref, vdt_ref, ckv_ref, ckvt_ref, kpet_ref, kvm_ref, qm_ref):
    h = _rms(x_ref[...], g_ref[...]).astype(BF16)
    cos = cos_ref[...]
    sin = sin_ref[...]
    lane = lax.broadcasted_iota(jnp.int32, cos.shape, 1)
    first_half = (lane & (DIFF_HEAD_DIM - 1)) < DIFF_HEAD_DIM // 2

    def rope(z):
        swapped = jnp.where(first_half, pltpu.roll(z, LANES - 32, 1), pltpu.roll(z, 32, 1))
        return z * cos + swapped * sin

    c0 = 0
    zq = _dot(h, win_ref[:, c0:c0 + DIFF_Q_COLS])
    diff_scale = DIFF_HEAD_DIM ** -0.5 * LOG2E
    for j in range(DIFF_Q_COLS // LANES):
        sl = slice(j * LANES, (j + 1) * LANES)
        qd_ref[:, sl] = (rope(zq[:, sl]) * diff_scale).astype(BF16)
    c0 += DIFF_Q_COLS

    zk = _dot(h, win_ref[:, c0:c0 + DIFF_K_COLS])
    for j in range(DIFF_K_COLS // LANES):
        sl = slice(j * LANES, (j + 1) * LANES)
        r = rope(zk[:, sl])
        kdt_ref[0, sl, :] = r.T
        kdb_ref[:, sl] = r.astype(BF16)
    c0 += DIFF_K_COLS

    zv = _dot(h, win_ref[:, c0:c0 + DIFF_V_COLS])
    vdf_ref[...] = zv
    vdt_ref[0] = zv.T.astype(BF16)
    c0 += DIFF_V_COLS

    cq = _rms(_dot(h, win_ref[:, c0:c0 + Q_LORA_RANK]), qn_ref[...]).astype(BF16)
    c0 += Q_LORA_RANK
    qm = _dot(cq, wuq_ref[...])
    mla_scale = (QK_NOPE_DIM + QK_ROPE_DIM) ** -0.5 * LOG2E
    pe0 = MLA_HEADS * QK_NOPE_DIM
    for hh in range(MLA_HEADS):
        q_nope = qm[:, hh * QK_NOPE_DIM:(hh + 1) * QK_NOPE_DIM].astype(BF16)
        q_lat = _dot(q_nope, wukt_ref[hh])
        qm_ref[hh, :, 0:KV_LORA_RANK] = (q_lat * mla_scale).astype(BF16)
        q_pe = rope(qm[:, pe0 + hh * LANES:pe0 + (hh + 1) * LANES])
        qm_ref[hh, :, KV_LORA_RANK:MLA_KEY_COLS] = (q_pe * mla_scale).astype(BF16)

    ckv = _rms(_dot(h, win_ref[:, c0:c0 + KV_LORA_RANK]), kvn_ref[...])
    ckv_ref[...] = ckv
    kvm_ref[:, 0:KV_LORA_RANK] = ckv.astype(BF16)
    ckvt_ref[0] = ckv.T.astype(BF16)
    c0 += KV_LORA_RANK

    kpe = rope(_dot(h, win_ref[:, c0:c0 + LANES]))
    kpet_ref[0] = kpe.T[0:QK_ROPE_DIM]
    kvm_ref[:, KV_LORA_RANK:MLA_KEY_COLS] = kpe.astype(BF16)


def _proj(x, gain, w_in, q_norm, w_uq, kv_norm, w_ukt, cos, sin, *, tm, seq):
    m, d = x.shape
    nt = seq // tm
    row = lambda i: (i, 0)
    chunk = lambda i: (i, 0, 0)
    seq_cols = lambda i: (i // nt, 0, i % nt)
    const2 = lambda i: (0, 0)
    const3 = lambda i: (0, 0, 0)
    out_shape = (
        jax.ShapeDtypeStruct((m, DIFF_Q_COLS), BF16),
        jax.ShapeDtypeStruct((m // seq, DIFF_K_COLS, seq), F32),
        jax.ShapeDtypeStruct((m, DIFF_K_COLS), BF16),
        jax.ShapeDtypeStruct((m, DIFF_V_COLS), F32),
        jax.ShapeDtypeStruct((m // tm, DIFF_V_COLS, tm), BF16),
        jax.ShapeDtypeStruct((m, KV_LORA_RANK), F32),
        jax.ShapeDtypeStruct((m // tm, KV_LORA_RANK, tm), BF16),
        jax.ShapeDtypeStruct((m // seq, QK_ROPE_DIM, seq), F32),
        jax.ShapeDtypeStruct((m, MLA_KEY_COLS), BF16),
        jax.ShapeDtypeStruct((MLA_HEADS, m, MLA_KEY_COLS), BF16),
    )
    out_specs = (
        pl.BlockSpec((tm, DIFF_Q_COLS), row),
        pl.BlockSpec((1, DIFF_K_COLS, tm), seq_cols),
        pl.BlockSpec((tm, DIFF_K_COLS), row),
        pl.BlockSpec((tm, DIFF_V_COLS), row),
        pl.BlockSpec((1, DIFF_V_COLS, tm), chunk),
        pl.BlockSpec((tm, KV_LORA_RANK), row),
        pl.BlockSpec((1, KV_LORA_RANK, tm), chunk),
        pl.BlockSpec((1, QK_ROPE_DIM, tm), seq_cols),
        pl.BlockSpec((tm, MLA_KEY_COLS), row),
        pl.BlockSpec((MLA_HEADS, tm, MLA_KEY_COLS), lambda i: (0, i, 0)),
    )
    return pl.pallas_call(
        _proj_body,
        out_shape=out_shape,
        grid=(m // tm,),
        in_specs=[pl.BlockSpec((tm, d), row),
                  pl.BlockSpec((1, d), const2),
                  pl.BlockSpec(w_in.shape, const2),
                  pl.BlockSpec((1, Q_LORA_RANK), const2),
                  pl.BlockSpec(w_uq.shape, const2),
                  pl.BlockSpec((1, KV_LORA_RANK), const2),
                  pl.BlockSpec(w_ukt.shape, const3),
                  pl.BlockSpec((tm, LANES), row),
                  pl.BlockSpec((tm, LANES), row)],
        out_specs=out_specs,
        compiler_params=_params(("parallel",)),
        name="proj",
    )(x, gain, w_in, q_norm, w_uq, kv_norm, w_ukt, cos, sin)


def _diff_lambda(lq1, lk1, lq2, lk2, lam_init):
    a = jnp.exp(jnp.sum(lq1[...] * lk1[...], axis=-1, keepdims=True))
    b = jnp.exp(jnp.sum(lq2[...] * lk2[...], axis=-1, keepdims=True))
    return a - b + lam_init


def _online_update(s, m_ref, l_ref):
    m_prev = m_ref[...]
    m_new = jnp.maximum(m_prev, jnp.max(s, axis=-1, keepdims=True))
    alpha = jnp.exp2(m_prev - m_new)
    p = jnp.exp2(s - m_new)
    l_ref[...] = alpha * l_ref[...] + jnp.sum(p, axis=-1, keepdims=True)
    m_ref[...] = m_new
    return alpha, p


def _init_softmax_state(m_ref, l_ref, acc_ref):
    m_ref[...] = jnp.full_like(m_ref, MASK_VALUE)
    l_ref[...] = jnp.zeros_like(l_ref)
    acc_ref[...] = jnp.zeros_like(acc_ref)


ONES_ROWS = 16


def _flash_update_t(s, vt, ones, m_ref, acc_ref):
    m_prev = m_ref[...]
    m_new = jnp.maximum(m_prev, jnp.max(s, axis=0, keepdims=True))
    alpha = jnp.exp2(m_prev - m_new)
    p = jnp.exp2((s - m_new).astype(BF16))
    acc_ref[...] = alpha * acc_ref[...] + _dot(jnp.concatenate([vt, ones], axis=0), p)
    m_ref[...] = m_new


def _init_flash_state_t(m_ref, acc_ref):
    m_ref[...] = jnp.full_like(m_ref, MASK_VALUE)
    acc_ref[...] = jnp.zeros_like(acc_ref)


def _causal_key_loop(step, qi, tq, tk):
    def body(j2, carry):
        step(2 * j2, False)
        step(2 * j2 + 1, False)
        return carry

    n_full = (qi * tq) // tk
    lax.fori_loop(0, n_full // 2, body, 0)

    @pl.when(n_full % 2 == 1)
    def _():
        step(n_full - 1, False)

    step(n_full, True)


def _causal_mask_t(shape, key_start, q_start, tq):
    key = lax.broadcasted_iota(jnp.int32, shape, 0) + key_start
    query = (lax.broadcasted_iota(jnp.int32, shape, 1) & (tq - 1)) + q_start
    return key <= query


def _diff_flash_body(q_ref, k_ref, vt_ref, lq1, lk1, lq2, lk2, subln_ref, o_ref,
                     m_ref, acc_ref, *, tq, tk, lam_init):
    qi = pl.program_id(2)
    chunks = tk // vt_ref.shape[2]
    q = q_ref[0]
    lane = lax.broadcasted_iota(jnp.int32, (tq, LANES), 1)
    comp0 = lane < DIFF_HEAD_DIM
    qa, qb = q[:, 0:LANES], q[:, LANES:2 * LANES]
    zero = jnp.zeros_like(qa)
    q4 = jnp.concatenate([jnp.where(comp0, qa, zero), jnp.where(comp0, qb, zero),
                          jnp.where(comp0, zero, qa), jnp.where(comp0, zero, qb)], axis=0)
    ones = jnp.ones((ONES_ROWS, tk), BF16)

    _init_flash_state_t(m_ref, acc_ref)

    def step(j, masked):
        start = pl.multiple_of(j * tk, tk)
        k = k_ref[0, pl.ds(start, tk), :]
        vt = jnp.concatenate([vt_ref[j * chunks + c] for c in range(chunks)], axis=1)
        s = _dot_nt(k, q4)
        if masked:
            s = jnp.where(_causal_mask_t(s.shape, start, qi * tq, tq), s, MASK_VALUE)
        _flash_update_t(s, vt, ones, m_ref, acc_ref)

    _causal_key_loop(step, qi, tq, tk)

    acc = acc_ref[...]
    o = acc[0:DIFF_V_DIM] / acc[DIFF_V_DIM:DIFF_V_DIM + 1]
    lam = _diff_lambda(lq1, lk1, lq2, lk2, lam_init)
    a = o[:, 0:2 * tq] - lam * o[:, 2 * tq:4 * tq]
    y = _rms(a, subln_ref[...], axis=0) * (1.0 - lam_init)
    o_ref[0, :, 0:LANES] = y[:, 0:tq].T.astype(BF16)
    o_ref[0, :, LANES:2 * LANES] = y[:, tq:2 * tq].T.astype(BF16)


def _diff_flash(qd, kd, vdt, lq1, lk1, lq2, lk2, subln_col, *, tq, tk, lam_init):
    b, t, _ = qd.shape
    n_chunk, _, chunk = vdt.shape
    per_batch = n_chunk // b
    const = lambda bi, ki, qi: (0, 0)
    lam_spec = pl.BlockSpec((1, DIFF_HEAD_DIM), const)
    return pl.pallas_call(
        functools.partial(_diff_flash_body, tq=tq, tk=tk, lam_init=lam_init),
        out_shape=jax.ShapeDtypeStruct((b, t, DIFF_HEADS * DIFF_V_DIM), BF16),
        grid=(b, DIFF_KV_HEADS, t // tq),
        in_specs=[pl.BlockSpec((1, tq, 2 * LANES), lambda bi, ki, qi: (bi, qi, ki)),
                  pl.BlockSpec((1, t, LANES), lambda bi, ki, qi: (bi, 0, ki)),
                  pl.BlockSpec((per_batch, DIFF_V_DIM, chunk), lambda bi, ki, qi: (bi, ki, 0)),
                  lam_spec, lam_spec, lam_spec, lam_spec,
                  pl.BlockSpec((DIFF_V_DIM, 1), const)],
        out_specs=pl.BlockSpec((1, tq, 2 * LANES), lambda bi, ki, qi: (bi, qi, ki)),
        scratch_shapes=[pltpu.VMEM((1, 4 * tq), F32),
                        pltpu.VMEM((DIFF_V_DIM + ONES_ROWS, 4 * tq), F32)],
        compiler_params=_params(("parallel", "parallel", "arbitrary")),
        name="diff_flash",
    )(qd, kd, vdt, lq1, lk1, lq2, lk2, subln_col)


def _mla_flash_body(q_ref, kv_ref, ckvt_ref, wuvt_ref, o_ref, m_ref, acc_ref, *, tq, tk):
    qi = pl.program_id(1)
    chunks = tk // ckvt_ref.shape[2]
    q = jnp.concatenate([q_ref[hh] for hh in range(MLA_HEADS)], axis=0)
    ones = jnp.ones((ONES_ROWS, tk), BF16)

    _init_flash_state_t(m_ref, acc_ref)

    def step(j, masked):
        start = pl.multiple_of(j * tk, tk)
        kv = kv_ref[0, pl.ds(start, tk), :]
        ct = jnp.concatenate([ckvt_ref[j * chunks + c] for c in range(chunks)], axis=1)
        s = _dot_nt(kv, q)
        if masked:
            s = jnp.where(_causal_mask_t(s.shape, start, qi * tq, tq), s, MASK_VALUE)
        _flash_update_t(s, ct, ones, m_ref, acc_ref)

    _causal_key_loop(step, qi, tq, tk)

    acc = acc_ref[...]
    o = (acc[0:KV_LORA_RANK] / acc[KV_LORA_RANK:KV_LORA_RANK + 1]).astype(BF16)
    for hh in range(MLA_HEADS):
        out_t = _dot(wuvt_ref[hh], o[:, hh * tq:(hh + 1) * tq])
        o_ref[0, :, hh * V_HEAD_DIM:(hh + 1) * V_HEAD_DIM] = out_t.T.astype(BF16)


def _mla_flash(qm, kvm, ckvt, w_uvt, *, tq, tk):
    b, t, _ = kvm.shape
    n_chunk, _, chunk = ckvt.shape
    per_batch = n_chunk // b
    nq = t // tq
    return pl.pallas_call(
        functools.partial(_mla_flash_body, tq=tq, tk=tk),
        out_shape=jax.ShapeDtypeStruct((b, t, MLA_HEADS * V_HEAD_DIM), BF16),
        grid=(b, nq),
        in_specs=[pl.BlockSpec((MLA_HEADS, tq, MLA_KEY_COLS), lambda bi, qi: (0, bi * nq + qi, 0)),
                  pl.BlockSpec((1, t, MLA_KEY_COLS), lambda bi, qi: (bi, 0, 0)),
                  pl.BlockSpec((per_batch, KV_LORA_RANK, chunk), lambda bi, qi: (bi, 0, 0)),
                  pl.BlockSpec(w_uvt.shape, lambda bi, qi: (0, 0, 0))],
        out_specs=pl.BlockSpec((1, tq, MLA_HEADS * V_HEAD_DIM), lambda bi, qi: (bi, qi, 0)),
        scratch_shapes=[pltpu.VMEM((1, MLA_HEADS * tq), F32),
                        pltpu.VMEM((KV_LORA_RANK + ONES_ROWS, MLA_HEADS * tq), F32)],
        compiler_params=_params(("parallel", "arbitrary")),
        name="mla_flash",
    )(qm, kvm, ckvt, w_uvt)


def _decode_body(pt_ref, qbd_ref, qm_ref, knew_ref, vnew_ref, cnew_ref, pnew_ref,
                 lq1, lk1, lq2, lk2, subln_ref, *rest, pages, lam_init):
    del pt_ref
    k_refs = rest[0:pages]
    v_refs = rest[pages:2 * pages]
    c_refs = rest[2 * pages:3 * pages]
    p_refs = rest[3 * pages:4 * pages]
    od_ref, om_ref, md_ref, ld_ref, accd_ref, mm_ref, lm_ref, accm_ref = rest[4 * pages:]
    step = pl.program_id(1)
    n_rows_d = 2 * DIFF_HEADS
    page = c_refs[0].shape[0]

    @pl.when(step == 0)
    def _():
        _init_softmax_state(md_ref, ld_ref, accd_ref)
        _init_softmax_state(mm_ref, lm_ref, accm_ref)

    qbd = qbd_ref[0]
    qm = qm_ref[0]
    q_lat = qm[:, 0:KV_LORA_RANK]
    q_pe = qm[:, KV_LORA_RANK:KV_LORA_RANK + QK_ROPE_DIM]
    row_head = (lax.broadcasted_iota(jnp.int32, (n_rows_d, DIFF_V_DIM), 0) & (DIFF_HEADS - 1)) // DIFF_GROUP

    sd = jnp.concatenate([_dot(qbd, k_refs[i][...].astype(BF16)) for i in range(pages)], axis=1)
    alpha, p = _online_update(sd, md_ref, ld_ref)
    p = p.astype(BF16)
    pv = jnp.zeros((n_rows_d, DIFF_V_DIM), F32)
    for kk in range(DIFF_KV_HEADS):
        pv_k = _dot(p[:, 0:page], v_refs[0][pl.ds(kk, page, stride=DIFF_KV_HEADS), :].astype(BF16))
        for i in range(1, pages):
            pv_k += _dot(p[:, i * page:(i + 1) * page],
                         v_refs[i][pl.ds(kk, page, stride=DIFF_KV_HEADS), :].astype(BF16))
        pv += jnp.where(row_head == kk, pv_k, 0.0)
    accd_ref[...] = alpha * accd_ref[...] + pv

    cs = [c_refs[i][...].astype(BF16) for i in range(pages)]
    sm = jnp.concatenate([_dot_nt(q_lat, cs[i]) + _dot(q_pe, p_refs[i][...].astype(BF16))
                          for i in range(pages)], axis=1)
    alpha, p = _online_update(sm, mm_ref, lm_ref)
    p = p.astype(BF16)
    pv = _dot(p[:, 0:page], cs[0])
    for i in range(1, pages):
        pv += _dot(p[:, i * page:(i + 1) * page], cs[i])
    accm_ref[...] = alpha * accm_ref[...] + pv

    @pl.when(step == pl.num_programs(1) - 1)
    def _():
        round_bf16 = lambda a: a.astype(BF16).astype(F32)
        k_new = round_bf16(knew_ref[0])
        v_new = round_bf16(vnew_ref[0])
        s_new = jnp.sum(qbd.astype(F32) * k_new, axis=-1, keepdims=True)
        alpha, p_new = _online_update(s_new, md_ref, ld_ref)
        v_own = jnp.zeros((n_rows_d, DIFF_V_DIM), F32)
        for kk in range(DIFF_KV_HEADS):
            v_own += jnp.where(row_head == kk, v_new[:, kk * DIFF_V_DIM:(kk + 1) * DIFF_V_DIM], 0.0)
        o = (alpha * accd_ref[...] + round_bf16(p_new) * v_own) / ld_ref[...]
        lam = _diff_lambda(lq1, lk1, lq2, lk2, lam_init)
        a = o[0:DIFF_HEADS] - lam * o[DIFF_HEADS:n_rows_d]
        od_ref[0] = _rms(a, subln_ref[...]) * (1.0 - lam_init)

        c_new = round_bf16(cnew_ref[0])
        pe_new = round_bf16(pnew_ref[0])
        s_new = (jnp.sum(q_lat.astype(F32) * c_new, axis=-1, keepdims=True)
                 + jnp.sum(q_pe.astype(F32) * pe_new, axis=-1, keepdims=True))
        alpha, p_new = _online_update(s_new, mm_ref, lm_ref)
        om_ref[0] = (alpha * accm_ref[...] + round_bf16(p_new) * c_new) / lm_ref[...]


def _decode(page_table, qbd, qm, k_new, v_new, c_new, pe_new, lq1, lk1, lq2, lk2, subln,
            cache_kt, cache_v, cache_c, cache_pt, *, pages, lam_init):
    nb, n_pages = page_table.shape
    per_sample = lambda b, s, pt: (b, 0, 0)
    const = lambda b, s, pt: (0, 0)
    lam_spec = pl.BlockSpec((1, DIFF_HEAD_DIM), const)

    def paged(cache):
        return [pl.BlockSpec((None,) + cache.shape[1:],
                             functools.partial(lambda b, s, pt, i: (pt[b, s * pages + i], 0, 0), i=i))
                for i in range(pages)]

    n_rows_d = 2 * DIFF_HEADS
    grid_spec = pltpu.PrefetchScalarGridSpec(
        num_scalar_prefetch=1,
        grid=(nb, n_pages // pages),
        in_specs=[pl.BlockSpec((1, n_rows_d, DIFF_K_COLS), per_sample),
                  pl.BlockSpec((1, MLA_HEADS, MLA_KEY_COLS), per_sample),
                  pl.BlockSpec((1, 1, DIFF_K_COLS), per_sample),
                  pl.BlockSpec((1, 1, DIFF_V_COLS), per_sample),
                  pl.BlockSpec((1, 1, KV_LORA_RANK), per_sample),
                  pl.BlockSpec((1, 1, QK_ROPE_DIM), per_sample),
                  lam_spec, lam_spec, lam_spec, lam_spec,
                  pl.BlockSpec((1, DIFF_V_DIM), const)]
                 + paged(cache_kt) + paged(cache_v) + paged(cache_c) + paged(cache_pt),
        out_specs=(pl.BlockSpec((1, DIFF_HEADS, DIFF_V_DIM), per_sample),
                   pl.BlockSpec((1, MLA_HEADS, KV_LORA_RANK), per_sample)),
        scratch_shapes=[pltpu.VMEM((n_rows_d, 1), F32), pltpu.VMEM((n_rows_d, 1), F32),
                        pltpu.VMEM((n_rows_d, DIFF_V_DIM), F32),
                        pltpu.VMEM((MLA_HEADS, 1), F32), pltpu.VMEM((MLA_HEADS, 1), F32),
                        pltpu.VMEM((MLA_HEADS, KV_LORA_RANK), F32)],
    )
    return pl.pallas_call(
        functools.partial(_decode_body, pages=pages, lam_init=lam_init),
        out_shape=(jax.ShapeDtypeStruct((nb, DIFF_HEADS, DIFF_V_DIM), F32),
                   jax.ShapeDtypeStruct((nb, MLA_HEADS, KV_LORA_RANK), F32)),
        grid_spec=grid_spec,
        compiler_params=_params(("parallel", "arbitrary")),
        name="decode",
    )(page_table, qbd, qm, k_new, v_new, c_new, pe_new, lq1, lk1, lq2, lk2, subln,
      *([cache_kt] * pages), *([cache_v] * pages), *([cache_c] * pages), *([cache_pt] * pages))


def _uv_body(o_ref, wuv_ref, out_ref):
    for hh in range(MLA_HEADS):
        out_ref[:, hh * V_HEAD_DIM:(hh + 1) * V_HEAD_DIM] = _dot(
            o_ref[hh].astype(BF16), wuv_ref[hh]).astype(BF16)


def _uv(o_lat, w_uvh):
    _, m, _ = o_lat.shape
    return pl.pallas_call(
        _uv_body,
        out_shape=jax.ShapeDtypeStruct((m, MLA_HEADS * V_HEAD_DIM), BF16),
        name="uv",
    )(o_lat, w_uvh)


def _oproj_body(md_ref, mm_ref, wo_ref, x_ref, g_ref, o_ref):
    half = md_ref.shape[1]
    y = _dot(md_ref[...], wo_ref[0:half, :]) + _dot(mm_ref[...], wo_ref[half:2 * half, :])
    o_ref[...] = x_ref[...] + _rms(y, g_ref[...])


def _oproj(mixed_d, mixed_m, w_o, x, gain, *, tm):
    m, d = x.shape
    half = mixed_d.shape[1]
    row = lambda i: (i, 0)
    const = lambda i: (0, 0)
    return pl.pallas_call(
        _oproj_body,
        out_shape=jax.ShapeDtypeStruct((m, d), F32),
        grid=(m // tm,),
        in_specs=[pl.BlockSpec((tm, half), row),
                  pl.BlockSpec((tm, half), row),
                  pl.BlockSpec(w_o.shape, const),
                  pl.BlockSpec((tm, d), row),
                  pl.BlockSpec((1, d), const)],
        out_specs=pl.BlockSpec((tm, d), row),
        compiler_params=_params(("parallel",)),
        name="oproj",
    )(mixed_d, mixed_m, w_o, x, gain)


def _rope_tables(pos):
    half = DIFF_HEAD_DIM // 2
    inv = jnp.exp(-math.log(ROPE_THETA) * jnp.arange(half, dtype=F32) * (2.0 / DIFF_HEAD_DIM))
    ang = pos[:, None] * inv[None, :]
    cos, sin = jnp.cos(ang), jnp.sin(ang)
    return jnp.tile(cos, (1, 4)), jnp.concatenate([-sin, sin, -sin, sin], axis=1)


def kernel(x_prompt, x_sample, cache_diff_k, cache_diff_v, cache_mla_ckv, cache_mla_kpe, page_table, ln_ffn1_pre, ln_ffn1_post, ffn1_w_gate, ffn1_w_up, ffn1_w_down, ln_mix_pre, ln_mix_post, w_in, diff_lambda_q1, diff_lambda_k1, diff_lambda_q2, diff_lambda_k2, diff_subln, mla_q_norm, w_uq, mla_kv_norm, w_uk, w_uv, w_o, ln_ffn2_pre, ln_ffn2_post, ffn2_w_gate, ffn2_w_up, ffn2_w_down):
    bsz, t_p, d = x_prompt.shape
    nb, t_s, _ = x_sample.shape
    assert t_s == 1, "one new token per sample"
    depth = w_in.shape[0]
    n_phys, page = cache_diff_k.shape[1:3]
    past_len = page_table.shape[1] * page

    xp = x_prompt.reshape(bsz * t_p, d)
    xs = x_sample.reshape(nb * t_s, d)
    cos_p, sin_p = _rope_tables(jnp.tile(jnp.arange(t_p, dtype=F32), bsz))
    cos_s, sin_s = _rope_tables(jnp.full((nb,), past_len, dtype=F32))

    new_p = [[], [], [], []]
    new_s = [[], [], [], []]
    for l in range(depth):
        lam_init = 0.8 - 0.6 * math.exp(-0.3 * l)
        bf = lambda w: w.astype(BF16)
        ffn1 = (ln_ffn1_pre[l][None], ln_ffn1_post[l][None], ffn1_w_gate[l], ffn1_w_up[l], ffn1_w_down[l])
        ffn2 = (ln_ffn2_pre[l][None], ln_ffn2_post[l][None], ffn2_w_gate[l], ffn2_w_up[l], ffn2_w_down[l])
        w_in_p = bf(jnp.pad(w_in[l], ((0, 0), (0, LANES - QK_ROPE_DIM))))
        w_uq_h = w_uq[l].reshape(Q_LORA_RANK, MLA_HEADS, QK_NOPE_DIM + QK_ROPE_DIM)
        w_uq_p = bf(jnp.concatenate([
            w_uq_h[:, :, :QK_NOPE_DIM].reshape(Q_LORA_RANK, MLA_HEADS * QK_NOPE_DIM),
            jnp.pad(w_uq_h[:, :, QK_NOPE_DIM:], ((0, 0), (0, 0), (0, LANES - QK_ROPE_DIM))
                    ).reshape(Q_LORA_RANK, MLA_HEADS * LANES)], axis=1))
        w_ukt = bf(jnp.transpose(w_uk[l], (1, 2, 0)))
        w_uvh = bf(jnp.transpose(w_uv[l], (1, 0, 2)))
        w_uvt = bf(jnp.transpose(w_uv[l], (1, 2, 0)))
        w_o_b = bf(w_o[l])
        proj_w = (ln_mix_pre[l][None], w_in_p, mla_q_norm[l][None], w_uq_p, mla_kv_norm[l][None], w_ukt)
        lams = (diff_lambda_q1[l][None], diff_lambda_k1[l][None], diff_lambda_q2[l][None], diff_lambda_k2[l][None])
        subln = diff_subln[l][None]

        xp = _ffn(xp, *ffn1, tm=FFN_ROWS, tf=FFN_HIDDEN)
        xs = _ffn(xs, *ffn1, tm=nb, tf=FFN_HIDDEN)

        qd, kdt, kdb, vdf, vdt, ckv, ckvt, kpet, kvm, qm = _proj(xp, *proj_w, cos_p, sin_p, tm=PROJ_ROWS, seq=t_p)
        mixed_d = _diff_flash(qd.reshape(bsz, t_p, -1), kdb.reshape(bsz, t_p, -1), vdt, *lams,
                              diff_subln[l][:, None], tq=DIFF_Q_TILE, tk=KEY_TILE, lam_init=lam_init)
        mixed_m = _mla_flash(qm, kvm.reshape(bsz, t_p, -1), ckvt, w_uvt, tq=MLA_Q_TILE, tk=KEY_TILE)
        xp = _oproj(mixed_d.reshape(bsz * t_p, -1), mixed_m.reshape(bsz * t_p, -1), w_o_b, xp,
                    ln_mix_post[l][None], tm=OPROJ_ROWS)
        kd_p = jnp.transpose(kdt.reshape(bsz, DIFF_KV_HEADS, 2, DIFF_HEAD_DIM, t_p), (0, 4, 1, 2, 3))
        for acc, val in zip(new_p, (kd_p, vdf, ckv, jnp.transpose(kpet, (0, 2, 1)))):
            acc.append(val)

        qd, kdt, _, vdf, _, ckv, _, kpet, _, qm = _proj(xs, *proj_w, cos_s, sin_s, tm=nb, seq=nb)
        kdf, kpe = kdt[0].T, kpet[0].T
        q5 = qd.astype(F32).reshape(nb, DIFF_KV_HEADS, DIFF_GROUP, 2, DIFF_HEAD_DIM)
        qbd = jnp.einsum('bkgcd,kK,cC->bckgKCd', q5, jnp.eye(DIFF_KV_HEADS, dtype=F32), jnp.eye(2, dtype=F32))
        qbd = qbd.reshape(nb, 2 * DIFF_HEADS, DIFF_K_COLS).astype(BF16)
        cache_kt = jnp.transpose(cache_diff_k[l].reshape(n_phys, page, DIFF_K_COLS), (0, 2, 1))
        cache_v = cache_diff_v[l].reshape(n_phys, page * DIFF_KV_HEADS, DIFF_V_DIM)
        cache_pt = jnp.transpose(cache_mla_kpe[l], (0, 2, 1))
        o_d, o_lat = _decode(
            page_table, qbd, jnp.transpose(qm, (1, 0, 2)),
            kdf[:, None], vdf[:, None], ckv[:, None], kpe[:, None], *lams, subln,
            cache_kt, cache_v, cache_mla_ckv[l], cache_pt, pages=DECODE_PAGES, lam_init=lam_init)
        mixed_d = o_d.reshape(nb, DIFF_HEADS * DIFF_V_DIM).astype(BF16)
        mixed_m = _uv(jnp.transpose(o_lat, (1, 0, 2)), w_uvh)
        xs = _oproj(mixed_d, mixed_m, w_o_b, xs, ln_mix_post[l][None], tm=nb)
        for acc, val in zip(new_s, (kdf, vdf, ckv, kpe)):
            acc.append(val)

        xp = _ffn(xp, *ffn2, tm=FFN_ROWS, tf=FFN_HIDDEN)
        xs = _ffn(xs, *ffn2, tm=nb, tf=FFN_HIDDEN)

    kd_shape = (DIFF_KV_HEADS, 2, DIFF_HEAD_DIM)
    vd_shape = (DIFF_KV_HEADS, DIFF_V_DIM)
    stack = lambda vals, lead, tail: jnp.stack([v.reshape(*lead, *tail) for v in vals])
    lead_p, lead_s = (bsz, t_p), (nb, t_s)
    return (xp.reshape(bsz, t_p, d), xs.reshape(nb, t_s, d),
            stack(new_p[0], lead_p, kd_shape), stack(new_p[1], lead_p, vd_shape),
            stack(new_p[2], lead_p, (KV_LORA_RANK,)), stack(new_p[3], lead_p, (QK_ROPE_DIM,)),
            stack(new_s[0], lead_s, kd_shape), stack(new_s[1], lead_s, vd_shape),
            stack(new_s[2], lead_s, (KV_LORA_RANK,)), stack(new_s[3], lead_s, (QK_ROPE_DIM,)))
```

```python
import functools
import math

import jax
import jax.numpy as jnp
from jax import lax
from jax.experimental import pallas as pl
from jax.experimental.pallas import tpu as pltpu

F32 = jnp.float32
BF16 = jnp.bfloat16

EPS = 1e-6
ROPE_THETA = 10000.0
MASK_VALUE = -1e30
LOG2E = math.log2(math.e)

DIFF_HEADS = 8
DIFF_KV_HEADS = 4
DIFF_GROUP = DIFF_HEADS // DIFF_KV_HEADS
DIFF_HEAD_DIM = 64
DIFF_V_DIM = 2 * DIFF_HEAD_DIM
MLA_HEADS = 8
QK_NOPE_DIM = 128
QK_ROPE_DIM = 64
V_HEAD_DIM = 128
Q_LORA_RANK = 512
KV_LORA_RANK = 256

DIFF_Q_COLS = DIFF_HEADS * 2 * DIFF_HEAD_DIM
DIFF_K_COLS = DIFF_KV_HEADS * 2 * DIFF_HEAD_DIM
DIFF_V_COLS = DIFF_KV_HEADS * DIFF_V_DIM
MLA_KEY_COLS = KV_LORA_RANK + 2 * QK_ROPE_DIM

LANES = 128
V7X_VMEM_LIMIT = 56 * 1024 * 1024

FFN_ROWS, FFN_HIDDEN = 1024, 256
PROJ_ROWS = 256
OPROJ_ROWS = 256
DIFF_Q_TILE, MLA_Q_TILE = 256, 128
KEY_TILE = 2 * PROJ_ROWS
DECODE_PAGES = 16

_NT = (((1,), (1,)), ((), ()))


def _dot(a, b):
    return jnp.dot(a, b, preferred_element_type=F32)


def _dot_nt(a, b):
    return lax.dot_general(a, b, _NT, preferred_element_type=F32)


def _rms(x, g, axis=-1):
    return x * lax.rsqrt(jnp.mean(x * x, axis=axis, keepdims=True) + EPS) * g


def _params(sem, vmem=V7X_VMEM_LIMIT, flags=None):
    return pltpu.CompilerParams(dimension_semantics=sem, vmem_limit_bytes=vmem, flags=flags)


def _ffn_body(x_ref, pre_ref, post_ref, wg_ref, wu_ref, wd_ref, o_ref, h_ref, *, nf):
    f = pl.program_id(1)

    @pl.when(f == 0)
    def _():
        h_ref[...] = _rms(x_ref[...], pre_ref[...]).astype(BF16)
        o_ref[...] = jnp.zeros_like(o_ref)

    h = h_ref[...]
    g = _dot(h, wg_ref[...].astype(BF16))
    u = _dot(h, wu_ref[...].astype(BF16))
    a = (g / (1.0 + jnp.exp(-g))) * u
    o_ref[...] += _dot(a.astype(BF16), wd_ref[...].astype(BF16))

    @pl.when(f == nf - 1)
    def _():
        o_ref[...] = x_ref[...] + 0.5 * _rms(o_ref[...], post_ref[...])


def _ffn(x, pre, post, wg, wu, wd, *, tm, tf):
    m, d = x.shape
    nf = wg.shape[1] // tf
    row = lambda i, f: (i, 0)
    const = lambda i, f: (0, 0)
    return pl.pallas_call(
        functools.partial(_ffn_body, nf=nf),
        out_shape=jax.ShapeDtypeStruct((m, d), F32),
        grid=(m // tm, nf),
        in_specs=[pl.BlockSpec((tm, d), row, pipeline_mode=pl.Buffered(1)),
                  pl.BlockSpec((1, d), const),
                  pl.BlockSpec((1, d), const),
                  pl.BlockSpec((d, tf), lambda i, f: (0, f)),
                  pl.BlockSpec((d, tf), lambda i, f: (0, f)),
                  pl.BlockSpec((tf, d), lambda i, f: (f, 0))],
        out_specs=pl.BlockSpec((tm, d), row),
        scratch_shapes=[pltpu.VMEM((tm, d), BF16)],
        compiler_params=_params(("parallel", "arbitrary")),
        name="ffn",
    )(x, pre, post, wg, wu, wd)


def _proj_body(x_ref, g_ref, win_ref, qn_ref, wuq_ref, kvn_ref, wukt_ref, cos_ref, sin_ref,
               qd_ref, kdt_ref, kdb_ref, vdf_ref, vdt_ref, ckv_ref, ckvt_ref, kpet_ref, kvm_ref, qm_ref):
    h = _rms(x_ref[...], g_ref[...]).astype(BF16)
    cos = cos_ref[...]
    sin = sin_ref[...]
    lane = lax.broadcasted_iota(jnp.int32, cos.shape, 1)
    first_half = (lane & (DIFF_HEAD_DIM - 1)) < DIFF_HEAD_DIM // 2

    def rope(z):
        swapped = jnp.where(first_half, pltpu.roll(z, LANES - 32, 1), pltpu.roll(z, 32, 1))
        return z * cos + swapped * sin

    c0 = 0
    zq = _dot(h, win_ref[:, c0:c0 + DIFF_Q_COLS])
    diff_scale = DIFF_HEAD_DIM ** -0.5 * LOG2E
    for j in range(DIFF_Q_COLS // LANES):
        sl = slice(j * LANES, (j + 1) * LANES)
        qd_ref[:, sl] = (rope(zq[:, sl]) * diff_scale).astype(BF16)
    c0 += DIFF_Q_COLS

    zk = _dot(h, win_ref[:, c0:c0 + DIFF_K_COLS])
    for j in range(DIFF_K_COLS // LANES):
        sl = slice(j * LANES, (j + 1) * LANES)
        r = rope(zk[:, sl])
        kdt_ref[0, sl, :] = r.T
        kdb_ref[:, sl] = r.astype(BF16)
    c0 += DIFF_K_COLS

    zv = _dot(h, win_ref[:, c0:c0 + DIFF_V_COLS])
    vdf_ref[...] = zv
    vdt_ref[0] = zv.T.astype(BF16)
    c0 += DIFF_V_COLS

    cq = _rms(_dot(h, win_ref[:, c0:c0 + Q_LORA_RANK]), qn_ref[...]).astype(BF16)
    c0 += Q_LORA_RANK
    qm = _dot(cq, wuq_ref[...])
    mla_scale = (QK_NOPE_DIM + QK_ROPE_DIM) ** -0.5 * LOG2E
    pe0 = MLA_HEADS * QK_NOPE_DIM
    for hh in range(MLA_HEADS):
        q_nope = qm[:, hh * QK_NOPE_DIM:(hh + 1) * QK_NOPE_DIM].astype(BF16)
        q_lat = _dot(q_nope, wukt_ref[hh])
        qm_ref[hh, :, 0:KV_LORA_RANK] = (q_lat * mla_scale).astype(BF16)
        q_pe = rope(qm[:, pe0 + hh * LANES:pe0 + (hh + 1) * LANES])
        qm_ref[hh, :, KV_LORA_RANK:MLA_KEY_COLS] = (q_pe * mla_scale).astype(BF16)

    ckv = _rms(_dot(h, win_ref[:, c0:c0 + KV_LORA_RANK]), kvn_ref[...])
    ckv_ref[...] = ckv
    kvm_ref[:, 0:KV_LORA_RANK] = ckv.astype(BF16)
    ckvt_ref[0] = ckv.T.astype(BF16)
    c0 += KV_LORA_RANK

    kpe = rope(_dot(h, win_ref[:, c0:c0 + LANES]))
    kpet_ref[0] = kpe.T[0:QK_ROPE_DIM]
    kvm_ref[:, KV_LORA_RANK:MLA_KEY_COLS] = kpe.astype(BF16)


def _proj(x, gain, w_in, q_norm, w_uq, kv_norm, w_ukt, cos, sin, *, tm, seq):
    m, d = x.shape
    nt = seq // tm
    row = lambda i: (i, 0)
    chunk = lambda i: (i, 0, 0)
    seq_cols = lambda i: (i // nt, 0, i % nt)
    const2 = lambda i: (0, 0)
    const3 = lambda i: (0, 0, 0)
    out_shape = (
        jax.ShapeDtypeStruct((m, DIFF_Q_COLS), BF16),
        jax.ShapeDtypeStruct((m // seq, DIFF_K_COLS, seq), F32),
        jax.ShapeDtypeStruct((m, DIFF_K_COLS), BF16),
        jax.ShapeDtypeStruct((m, DIFF_V_COLS), F32),
        jax.ShapeDtypeStruct((m // tm, DIFF_V_COLS, tm), BF16),
        jax.ShapeDtypeStruct((m, KV_LORA_RANK), F32),
        jax.ShapeDtypeStruct((m // tm, KV_LORA_RANK, tm), BF16),
        jax.ShapeDtypeStruct((m // seq, QK_ROPE_DIM, seq), F32),
        jax.ShapeDtypeStruct((m, MLA_KEY_COLS), BF16),
        jax.ShapeDtypeStruct((MLA_HEADS, m, MLA_KEY_COLS), BF16),
    )
    out_specs = (
        pl.BlockSpec((tm, DIFF_Q_COLS), row),
        pl.BlockSpec((1, DIFF_K_COLS, tm), seq_cols),
        pl.BlockSpec((tm, DIFF_K_COLS), row),
        pl.BlockSpec((tm, DIFF_V_COLS), row),
        pl.BlockSpec((1, DIFF_V_COLS, tm), chunk),
        pl.BlockSpec((tm, KV_LORA_RANK), row),
        pl.BlockSpec((1, KV_LORA_RANK, tm), chunk),
        pl.BlockSpec((1, QK_ROPE_DIM, tm), seq_cols),
        pl.BlockSpec((tm, MLA_KEY_COLS), row),
        pl.BlockSpec((MLA_HEADS, tm, MLA_KEY_COLS), lambda i: (0, i, 0)),
    )
    return pl.pallas_call(
        _proj_body,
        out_shape=out_shape,
        grid=(m // tm,),
        in_specs=[pl.BlockSpec((tm, d), row),
                  pl.BlockSpec((1, d), const2),
                  pl.BlockSpec(w_in.shape, const2),
                  pl.BlockSpec((1, Q_LORA_RANK), const2),
                  pl.BlockSpec(w_uq.shape, const2),
                  pl.BlockSpec((1, KV_LORA_RANK), const2),
                  pl.BlockSpec(w_ukt.shape, const3),
                  pl.BlockSpec((tm, LANES), row),
                  pl.BlockSpec((tm, LANES), row)],
        out_specs=out_specs,
        compiler_params=_params(("parallel",)),
        name="proj",
    )(x, gain, w_in, q_norm, w_uq, kv_norm, w_ukt, cos, sin)


def _diff_lambda(lq1, lk1, lq2, lk2, lam_init):
    a = jnp.exp(jnp.sum(lq1[...] * lk1[...], axis=-1, keepdims=True))
    b = jnp.exp(jnp.sum(lq2[...] * lk2[...], axis=-1, keepdims=True))
    return a - b + lam_init


def _online_update(s, m_ref, l_ref):
    m_prev = m_ref[...]
    m_new = jnp.maximum(m_prev, jnp.max(s, axis=-1, keepdims=True))
    alpha = jnp.exp2(m_prev - m_new)
    p = jnp.exp2(s - m_new)
    l_ref[...] = alpha * l_ref[...] + jnp.sum(p, axis=-1, keepdims=True)
    m_ref[...] = m_new
    return alpha, p


def _init_softmax_state(m_ref, l_ref, acc_ref):
    m_ref[...] = jnp.full_like(m_ref, MASK_VALUE)
    l_ref[...] = jnp.zeros_like(l_ref)
    acc_ref[...] = jnp.zeros_like(acc_ref)


ONES_ROWS = 16


def _flash_update_t(s, vt, ones, m_ref, acc_ref):
    m_prev = m_ref[...]
    m_new = jnp.maximum(m_prev, jnp.max(s, axis=0, keepdims=True))
    alpha = jnp.exp2(m_prev - m_new)
    p = jnp.exp2(s - m_new).astype(BF16)
    acc_ref[...] = alpha * acc_ref[...] + _dot(jnp.concatenate([vt, ones], axis=0), p)
    m_ref[...] = m_new


def _init_flash_state_t(m_ref, acc_ref):
    m_ref[...] = jnp.full_like(m_ref, MASK_VALUE)
    acc_ref[...] = jnp.zeros_like(acc_ref)


def _causal_key_loop(step, qi, tq, tk):
    def body(j, carry):
        step(j, False)
        return carry

    n_full = (qi * tq) // tk
    lax.fori_loop(0, n_full, body, 0)
    step(n_full, True)


def _causal_mask_t(shape, key_start, q_start, tq):
    key = lax.broadcasted_iota(jnp.int32, shape, 0) + key_start
    query = (lax.broadcasted_iota(jnp.int32, shape, 1) & (tq - 1)) + q_start
    return key <= query


def _diff_flash_body(q_ref, k_ref, vt_ref, lq1, lk1, lq2, lk2, subln_ref, o_ref,
                     m_ref, acc_ref, *, tq, tk, lam_init):
    qi = pl.program_id(2)
    chunks = tk // vt_ref.shape[2]
    q = q_ref[0]
    lane = lax.broadcasted_iota(jnp.int32, (tq, LANES), 1)
    comp0 = lane < DIFF_HEAD_DIM
    qa, qb = q[:, 0:LANES], q[:, LANES:2 * LANES]
    zero = jnp.zeros_like(qa)
    q4 = jnp.concatenate([jnp.where(comp0, qa, zero), jnp.where(comp0, qb, zero),
                          jnp.where(comp0, zero, qa), jnp.where(comp0, zero, qb)], axis=0)
    ones = jnp.ones((ONES_ROWS, tk), BF16)

    _init_flash_state_t(m_ref, acc_ref)

    def step(j, masked):
        start = pl.multiple_of(j * tk, tk)
        k = k_ref[0, pl.ds(start, tk), :]
        vt = jnp.concatenate([vt_ref[j * chunks + c] for c in range(chunks)], axis=1)
        s = _dot_nt(k, q4)
        if masked:
            s = jnp.where(_causal_mask_t(s.shape, start, qi * tq, tq), s, MASK_VALUE)
        _flash_update_t(s, vt, ones, m_ref, acc_ref)

    _causal_key_loop(step, qi, tq, tk)

    acc = acc_ref[...]
    o = acc[0:DIFF_V_DIM] / acc[DIFF_V_DIM:DIFF_V_DIM + 1]
    lam = _diff_lambda(lq1, lk1, lq2, lk2, lam_init)
    a = o[:, 0:2 * tq] - lam * o[:, 2 * tq:4 * tq]
    y = _rms(a, subln_ref[...], axis=0) * (1.0 - lam_init)
    o_ref[0, :, 0:LANES] = y[:, 0:tq].T.astype(BF16)
    o_ref[0, :, LANES:2 * LANES] = y[:, tq:2 * tq].T.astype(BF16)


def _diff_flash(qd, kd, vdt, lq1, lk1, lq2, lk2, subln_col, *, tq, tk, lam_init):
    b, t, _ = qd.shape
    n_chunk, _, chunk = vdt.shape
    per_batch = n_chunk // b
    const = lambda bi, ki, qi: (0, 0)
    lam_spec = pl.BlockSpec((1, DIFF_HEAD_DIM), const)
    return pl.pallas_call(
        functools.partial(_diff_flash_body, tq=tq, tk=tk, lam_init=lam_init),
        out_shape=jax.ShapeDtypeStruct((b, t, DIFF_HEADS * DIFF_V_DIM), BF16),
        grid=(b, DIFF_KV_HEADS, t // tq),
        in_specs=[pl.BlockSpec((1, tq, 2 * LANES), lambda bi, ki, qi: (bi, qi, ki)),
                  pl.BlockSpec((1, t, LANES), lambda bi, ki, qi: (bi, 0, ki)),
                  pl.BlockSpec((per_batch, DIFF_V_DIM, chunk), lambda bi, ki, qi: (bi, ki, 0)),
                  lam_spec, lam_spec, lam_spec, lam_spec,
                  pl.BlockSpec((DIFF_V_DIM, 1), const)],
        out_specs=pl.BlockSpec((1, tq, 2 * LANES), lambda bi, ki, qi: (bi, qi, ki)),
        scratch_shapes=[pltpu.VMEM((1, 4 * tq), F32),
                        pltpu.VMEM((DIFF_V_DIM + ONES_ROWS, 4 * tq), F32)],
        compiler_params=_params(("parallel", "parallel", "arbitrary")),
        name="diff_flash",
    )(qd, kd, vdt, lq1, lk1, lq2, lk2, subln_col)


def _mla_flash_body(q_ref, kv_ref, ckvt_ref, wuvt_ref, o_ref, m_ref, acc_ref, *, tq, tk):
    qi = pl.program_id(1)
    chunks = tk // ckvt_ref.shape[2]
    q = jnp.concatenate([q_ref[hh] for hh in range(MLA_HEADS)], axis=0)
    ones = jnp.ones((ONES_ROWS, tk), BF16)

    _init_flash_state_t(m_ref, acc_ref)

    def step(j, masked):
        start = pl.multiple_of(j * tk, tk)
        kv = kv_ref[0, pl.ds(start, tk), :]
        ct = jnp.concatenate([ckvt_ref[j * chunks + c] for c in range(chunks)], axis=1)
        s = _dot_nt(kv, q)
        if masked:
            s = jnp.where(_causal_mask_t(s.shape, start, qi * tq, tq), s, MASK_VALUE)
        _flash_update_t(s, ct, ones, m_ref, acc_ref)

    _causal_key_loop(step, qi, tq, tk)

    acc = acc_ref[...]
    o = (acc[0:KV_LORA_RANK] / acc[KV_LORA_RANK:KV_LORA_RANK + 1]).astype(BF16)
    for hh in range(MLA_HEADS):
        out_t = _dot(wuvt_ref[hh], o[:, hh * tq:(hh + 1) * tq])
        o_ref[0, :, hh * V_HEAD_DIM:(hh + 1) * V_HEAD_DIM] = out_t.T.astype(BF16)


def _mla_flash(qm, kvm, ckvt, w_uvt, *, tq, tk):
    b, t, _ = kvm.shape
    n_chunk, _, chunk = ckvt.shape
    per_batch = n_chunk // b
    nq = t // tq
    return pl.pallas_call(
        functools.partial(_mla_flash_body, tq=tq, tk=tk),
        out_shape=jax.ShapeDtypeStruct((b, t, MLA_HEADS * V_HEAD_DIM), BF16),
        grid=(b, nq),
        in_specs=[pl.BlockSpec((MLA_HEADS, tq, MLA_KEY_COLS), lambda bi, qi: (0, bi * nq + qi, 0)),
                  pl.BlockSpec((1, t, MLA_KEY_COLS), lambda bi, qi: (bi, 0, 0)),
                  pl.BlockSpec((per_batch, KV_LORA_RANK, chunk), lambda bi, qi: (bi, 0, 0)),
                  pl.BlockSpec(w_uvt.shape, lambda bi, qi: (0, 0, 0))],
        out_specs=pl.BlockSpec((1, tq, MLA_HEADS * V_HEAD_DIM), lambda bi, qi: (bi, qi, 0)),
        scratch_shapes=[pltpu.VMEM((1, MLA_HEADS * tq), F32),
                        pltpu.VMEM((KV_LORA_RANK + ONES_ROWS, MLA_HEADS * tq), F32)],
        compiler_params=_params(("parallel", "arbitrary")),
        name="mla_flash",
    )(qm, kvm, ckvt, w_uvt)


def _decode_body(pt_ref, qbd_ref, qm_ref, knew_ref, vnew_ref, cnew_ref, pnew_ref,
                 lq1, lk1, lq2, lk2, subln_ref, ck_hbm, cv_hbm, cc_hbm, cp_hbm, od_ref, om_ref,
                 kbuf, vbuf, cbuf, pbuf, sems, md_ref, ld_ref, accd_ref, mm_ref, lm_ref, accm_ref,
                 *, pages, lam_init):
    b = pl.program_id(0)
    step = pl.program_id(1)
    n_steps = pl.num_programs(1)
    flat = b * n_steps + step
    slot = flat & 1
    n_rows_d = 2 * DIFF_HEADS
    page = cbuf.shape[2]
    caches = ((ck_hbm, kbuf), (cv_hbm, vbuf), (cc_hbm, cbuf), (cp_hbm, pbuf))

    def page_copy(cache, page_idx, slot_, i):
        hbm, buf = caches[cache]
        return pltpu.make_async_copy(hbm.at[page_idx], buf.at[slot_, i], sems.at[slot_, cache])

    def start_fetch(b_, step_, slot_):
        for i in range(pages):
            page_idx = pt_ref[b_, step_ * pages + i]
            for cache in range(len(caches)):
                page_copy(cache, page_idx, slot_, i).start()

    def wait_fetch(slot_):
        for i in range(pages):
            for cache in range(len(caches)):
                page_copy(cache, 0, slot_, i).wait()

    @pl.when(flat == 0)
    def _():
        start_fetch(0, 0, 0)

    @pl.when(flat + 1 < pl.num_programs(0) * n_steps)
    def _():
        wraps = step + 1 == n_steps
        start_fetch(jnp.where(wraps, b + 1, b), jnp.where(wraps, 0, step + 1), 1 - slot)

    wait_fetch(slot)
    k_refs = [kbuf.at[slot, i] for i in range(pages)]
    v_refs = [vbuf.at[slot, i] for i in range(pages)]
    c_refs = [cbuf.at[slot, i] for i in range(pages)]
    p_refs = [pbuf.at[slot, i] for i in range(pages)]

    @pl.when(step == 0)
    def _():
        _init_softmax_state(md_ref, ld_ref, accd_ref)
        _init_softmax_state(mm_ref, lm_ref, accm_ref)

    qbd = qbd_ref[0]
    qm = qm_ref[0]
    q_lat = qm[:, 0:KV_LORA_RANK]
    q_pe = qm[:, KV_LORA_RANK:KV_LORA_RANK + QK_ROPE_DIM]
    row_head = (lax.broadcasted_iota(jnp.int32, (n_rows_d, DIFF_V_DIM), 0) & (DIFF_HEADS - 1)) // DIFF_GROUP

    sd = jnp.concatenate([_dot(qbd, k_refs[i][...].astype(BF16)) for i in range(pages)], axis=1)
    alpha, p = _online_update(sd, md_ref, ld_ref)
    p = p.astype(BF16)
    pv = jnp.zeros((n_rows_d, DIFF_V_DIM), F32)
    for kk in range(DIFF_KV_HEADS):
        pv_k = _dot(p[:, 0:page], v_refs[0][pl.ds(kk, page, stride=DIFF_KV_HEADS), :].astype(BF16))
        for i in range(1, pages):
            pv_k += _dot(p[:, i * page:(i + 1) * page],
                         v_refs[i][pl.ds(kk, page, stride=DIFF_KV_HEADS), :].astype(BF16))
        pv += jnp.where(row_head == kk, pv_k, 0.0)
    accd_ref[...] = alpha * accd_ref[...] + pv

    cs = [c_refs[i][...].astype(BF16) for i in range(pages)]
    sm = jnp.concatenate([_dot_nt(q_lat, cs[i]) + _dot(q_pe, p_refs[i][...].astype(BF16))
                          for i in range(pages)], axis=1)
    alpha, p = _online_update(sm, mm_ref, lm_ref)
    p = p.astype(BF16)
    pv = _dot(p[:, 0:page], cs[0])
    for i in range(1, pages):
        pv += _dot(p[:, i * page:(i + 1) * page], cs[i])
    accm_ref[...] = alpha * accm_ref[...] + pv

    @pl.when(step == pl.num_programs(1) - 1)
    def _():
        round_bf16 = lambda a: a.astype(BF16).astype(F32)
        k_new = round_bf16(knew_ref[0])
        v_new = round_bf16(vnew_ref[0])
        s_new = jnp.sum(qbd.astype(F32) * k_new, axis=-1, keepdims=True)
        alpha, p_new = _online_update(s_new, md_ref, ld_ref)
        v_own = jnp.zeros((n_rows_d, DIFF_V_DIM), F32)
        for kk in range(DIFF_KV_HEADS):
            v_own += jnp.where(row_head == kk, v_new[:, kk * DIFF_V_DIM:(kk + 1) * DIFF_V_DIM], 0.0)
        o = (alpha * accd_ref[...] + round_bf16(p_new) * v_own) / ld_ref[...]
        lam = _diff_lambda(lq1, lk1, lq2, lk2, lam_init)
        a = o[0:DIFF_HEADS] - lam * o[DIFF_HEADS:n_rows_d]
        od_ref[0] = _rms(a, subln_ref[...]) * (1.0 - lam_init)

        c_new = round_bf16(cnew_ref[0])
        pe_new = round_bf16(pnew_ref[0])
        s_new = (jnp.sum(q_lat.astype(F32) * c_new, axis=-1, keepdims=True)
                 + jnp.sum(q_pe.astype(F32) * pe_new, axis=-1, keepdims=True))
        alpha, p_new = _online_update(s_new, mm_ref, lm_ref)
        om_ref[0] = (alpha * accm_ref[...] + round_bf16(p_new) * c_new) / lm_ref[...]


def _decode(page_table, qbd, qm, k_new, v_new, c_new, pe_new, lq1, lk1, lq2, lk2, subln,
            cache_kt, cache_v, cache_c, cache_pt, *, pages, lam_init):
    nb, n_pages = page_table.shape
    per_sample = lambda b, s, pt: (b, 0, 0)
    const = lambda b, s, pt: (0, 0)
    lam_spec = pl.BlockSpec((1, DIFF_HEAD_DIM), const)
    caches = (cache_kt, cache_v, cache_c, cache_pt)
    n_rows_d = 2 * DIFF_HEADS
    grid_spec = pltpu.PrefetchScalarGridSpec(
        num_scalar_prefetch=1,
        grid=(nb, n_pages // pages),
        in_specs=[pl.BlockSpec((1, n_rows_d, DIFF_K_COLS), per_sample),
                  pl.BlockSpec((1, MLA_HEADS, MLA_KEY_COLS), per_sample),
                  pl.BlockSpec((1, 1, DIFF_K_COLS), per_sample),
                  pl.BlockSpec((1, 1, DIFF_V_COLS), per_sample),
                  pl.BlockSpec((1, 1, KV_LORA_RANK), per_sample),
                  pl.BlockSpec((1, 1, QK_ROPE_DIM), per_sample),
                  lam_spec, lam_spec, lam_spec, lam_spec,
                  pl.BlockSpec((1, DIFF_V_DIM), const)]
                 + [pl.BlockSpec(memory_space=pl.ANY) for _ in caches],
        out_specs=(pl.BlockSpec((1, DIFF_HEADS, DIFF_V_DIM), per_sample),
                   pl.BlockSpec((1, MLA_HEADS, KV_LORA_RANK), per_sample)),
        scratch_shapes=[pltpu.VMEM((2, pages) + c.shape[1:], c.dtype) for c in caches]
                       + [pltpu.SemaphoreType.DMA((2, len(caches))),
                          pltpu.VMEM((n_rows_d, 1), F32), pltpu.VMEM((n_rows_d, 1), F32),
                        pltpu.VMEM((n_rows_d, DIFF_V_DIM), F32),
                        pltpu.VMEM((MLA_HEADS, 1), F32), pltpu.VMEM((MLA_HEADS, 1), F32),
                        pltpu.VMEM((MLA_HEADS, KV_LORA_RANK), F32)],
    )
    return pl.pallas_call(
        functools.partial(_decode_body, pages=pages, lam_init=lam_init),
        out_shape=(jax.ShapeDtypeStruct((nb, DIFF_HEADS, DIFF_V_DIM), F32),
                   jax.ShapeDtypeStruct((nb, MLA_HEADS, KV_LORA_RANK), F32)),
        grid_spec=grid_spec,
        compiler_params=_params(("arbitrary", "arbitrary")),
        name="decode",
    )(page_table, qbd, qm, k_new, v_new, c_new, pe_new, lq1, lk1, lq2, lk2, subln, *caches)


def _uv_body(o_ref, wuv_ref, out_ref):
    for hh in range(MLA_HEADS):
        out_ref[:, hh * V_HEAD_DIM:(hh + 1) * V_HEAD_DIM] = _dot(
            o_ref[hh].astype(BF16), wuv_ref[hh]).astype(BF16)


def _uv(o_lat, w_uvh):
    _, m, _ = o_lat.shape
    return pl.pallas_call(
        _uv_body,
        out_shape=jax.ShapeDtypeStruct((m, MLA_HEADS * V_HEAD_DIM), BF16),
        name="uv",
    )(o_lat, w_uvh)


def _oproj_body(md_ref, mm_ref, wo_ref, x_ref, g_ref, o_ref):
    half = md_ref.shape[1]
    y = _dot(md_ref[...], wo_ref[0:half, :]) + _dot(mm_ref[...], wo_ref[half:2 * half, :])
    o_ref[...] = x_ref[...] + _rms(y, g_ref[...])


def _oproj(mixed_d, mixed_m, w_o, x, gain, *, tm):
    m, d = x.shape
    half = mixed_d.shape[1]
    row = lambda i: (i, 0)
    const = lambda i: (0, 0)
    return pl.pallas_call(
        _oproj_body,
        out_shape=jax.ShapeDtypeStruct((m, d), F32),
        grid=(m // tm,),
        in_specs=[pl.BlockSpec((tm, half), row),
                  pl.BlockSpec((tm, half), row),
                  pl.BlockSpec(w_o.shape, const),
                  pl.BlockSpec((tm, d), row),
                  pl.BlockSpec((1, d), const)],
        out_specs=pl.BlockSpec((tm, d), row),
        compiler_params=_params(("parallel",)),
        name="oproj",
    )(mixed_d, mixed_m, w_o, x, gain)


def _rope_tables(pos):
    half = DIFF_HEAD_DIM // 2
    inv = jnp.exp(-math.log(ROPE_THETA) * jnp.arange(half, dtype=F32) * (2.0 / DIFF_HEAD_DIM))
    ang = pos[:, None] * inv[None, :]
    cos, sin = jnp.cos(ang), jnp.sin(ang)
    return jnp.tile(cos, (1, 4)), jnp.concatenate([-sin, sin, -sin, sin], axis=1)


def kernel(x_prompt, x_sample, cache_diff_k, cache_diff_v, cache_mla_ckv, cache_mla_kpe, page_table, ln_ffn1_pre, ln_ffn1_post, ffn1_w_gate, ffn1_w_up, ffn1_w_down, ln_mix_pre, ln_mix_post, w_in, diff_lambda_q1, diff_lambda_k1, diff_lambda_q2, diff_lambda_k2, diff_subln, mla_q_norm, w_uq, mla_kv_norm, w_uk, w_uv, w_o, ln_ffn2_pre, ln_ffn2_post, ffn2_w_gate, ffn2_w_up, ffn2_w_down):
    bsz, t_p, d = x_prompt.shape
    nb, t_s, _ = x_sample.shape
    assert t_s == 1, "one new token per sample"
    depth = w_in.shape[0]
    n_phys, page = cache_diff_k.shape[1:3]
    past_len = page_table.shape[1] * page

    xp = x_prompt.reshape(bsz * t_p, d)
    xs = x_sample.reshape(nb * t_s, d)
    cos_p, sin_p = _rope_tables(jnp.tile(jnp.arange(t_p, dtype=F32), bsz))
    cos_s, sin_s = _rope_tables(jnp.full((nb,), past_len, dtype=F32))

    new_p = [[], [], [], []]
    new_s = [[], [], [], []]
    for l in range(depth):
        lam_init = 0.8 - 0.6 * math.exp(-0.3 * l)
        bf = lambda w: w.astype(BF16)
        ffn1 = (ln_ffn1_pre[l][None], ln_ffn1_post[l][None], ffn1_w_gate[l], ffn1_w_up[l], ffn1_w_down[l])
        ffn2 = (ln_ffn2_pre[l][None], ln_ffn2_post[l][None], ffn2_w_gate[l], ffn2_w_up[l], ffn2_w_down[l])
        w_in_p = bf(jnp.pad(w_in[l], ((0, 0), (0, LANES - QK_ROPE_DIM))))
        w_uq_h = w_uq[l].reshape(Q_LORA_RANK, MLA_HEADS, QK_NOPE_DIM + QK_ROPE_DIM)
        w_uq_p = bf(jnp.concatenate([
            w_uq_h[:, :, :QK_NOPE_DIM].reshape(Q_LORA_RANK, MLA_HEADS * QK_NOPE_DIM),
            jnp.pad(w_uq_h[:, :, QK_NOPE_DIM:], ((0, 0), (0, 0), (0, LANES - QK_ROPE_DIM))
                    ).reshape(Q_LORA_RANK, MLA_HEADS * LANES)], axis=1))
        w_ukt = bf(jnp.transpose(w_uk[l], (1, 2, 0)))
        w_uvh = bf(jnp.transpose(w_uv[l], (1, 0, 2)))
        w_uvt = bf(jnp.transpose(w_uv[l], (1, 2, 0)))
        w_o_b = bf(w_o[l])
        proj_w = (ln_mix_pre[l][None], w_in_p, mla_q_norm[l][None], w_uq_p, mla_kv_norm[l][None], w_ukt)
        lams = (diff_lambda_q1[l][None], diff_lambda_k1[l][None], diff_lambda_q2[l][None], diff_lambda_k2[l][None])
        subln = diff_subln[l][None]

        xp = _ffn(xp, *ffn1, tm=FFN_ROWS, tf=FFN_HIDDEN)
        xs = _ffn(xs, *ffn1, tm=nb, tf=FFN_HIDDEN)

        qd, kdt, kdb, vdf, vdt, ckv, ckvt, kpet, kvm, qm = _proj(xp, *proj_w, cos_p, sin_p, tm=PROJ_ROWS, seq=t_p)
        mixed_d = _diff_flash(qd.reshape(bsz, t_p, -1), kdb.reshape(bsz, t_p, -1), vdt, *lams,
                              diff_subln[l][:, None], tq=DIFF_Q_TILE, tk=KEY_TILE, lam_init=lam_init)
        mixed_m = _mla_flash(qm, kvm.reshape(bsz, t_p, -1), ckvt, w_uvt, tq=MLA_Q_TILE, tk=KEY_TILE)
        xp = _oproj(mixed_d.reshape(bsz * t_p, -1), mixed_m.reshape(bsz * t_p, -1), w_o_b, xp,
                    ln_mix_post[l][None], tm=OPROJ_ROWS)
        kd_p = jnp.transpose(kdt.reshape(bsz, DIFF_KV_HEADS, 2, DIFF_HEAD_DIM, t_p), (0, 4, 1, 2, 3))
        for acc, val in zip(new_p, (kd_p, vdf, ckv, jnp.transpose(kpet, (0, 2, 1)))):
            acc.append(val)

        qd, kdt, _, vdf, _, ckv, _, kpet, _, qm = _proj(xs, *proj_w, cos_s, sin_s, tm=nb, seq=nb)
        kdf, kpe = kdt[0].T, kpet[0].T
        q5 = qd.astype(F32).reshape(nb, DIFF_KV_HEADS, DIFF_GROUP, 2, DIFF_HEAD_DIM)
        qbd = jnp.einsum('bkgcd,kK,cC->bckgKCd', q5, jnp.eye(DIFF_KV_HEADS, dtype=F32), jnp.eye(2, dtype=F32))
        qbd = qbd.reshape(nb, 2 * DIFF_HEADS, DIFF_K_COLS).astype(BF16)
        cache_kt = jnp.transpose(cache_diff_k[l].reshape(n_phys, page, DIFF_K_COLS), (0, 2, 1))
        cache_v = cache_diff_v[l].reshape(n_phys, page * DIFF_KV_HEADS, DIFF_V_DIM)
        cache_pt = jnp.transpose(cache_mla_kpe[l], (0, 2, 1))
        o_d, o_lat = _decode(
            page_table, qbd, jnp.transpose(qm, (1, 0, 2)),
            kdf[:, None], vdf[:, None], ckv[:, None], kpe[:, None], *lams, subln,
            cache_kt, cache_v, cache_mla_ckv[l], cache_pt, pages=DECODE_PAGES, lam_init=lam_init)
        mixed_d = o_d.reshape(nb, DIFF_HEADS * DIFF_V_DIM).astype(BF16)
        mixed_m = _uv(jnp.transpose(o_lat, (1, 0, 2)), w_uvh)
        xs = _oproj(mixed_d, mixed_m, w_o_b, xs, ln_mix_post[l][None], tm=nb)
        for acc, val in zip(new_s, (kdf, vdf, ckv, kpe)):
            acc.append(val)

        xp = _ffn(xp, *ffn2, tm=FFN_ROWS, tf=FFN_HIDDEN)
        xs = _ffn(xs, *ffn2, tm=nb, tf=FFN_HIDDEN)

    kd_shape = (DIFF_KV_HEADS, 2, DIFF_HEAD_DIM)
    vd_shape = (DIFF_KV_HEADS, DIFF_V_DIM)
    stack = lambda vals, lead, tail: jnp.stack([v.reshape(*lead, *tail) for v in vals])
    lead_p, lead_s = (bsz, t_p), (nb, t_s)
    return (xp.reshape(bsz, t_p, d), xs.reshape(nb, t_s, d),
            stack(new_p[0], lead_p, kd_shape), stack(new_p[1], lead_p, vd_shape),
            stack(new_p[2], lead_p, (KV_LORA_RANK,)), stack(new_p[3], lead_p, (QK_ROPE_DIM,)),
            stack(new_s[0], lead_s, kd_shape), stack(new_s[1], lead_s, vd_shape),
            stack(new_s[2], lead_s, (KV_LORA_RANK,)), stack(new_s[3], lead_s, (QK_ROPE_DIM,)))
```

```python
import functools
import math

import jax
import jax.numpy as jnp
from jax import lax
from jax.experimental import pallas as pl
from jax.experimental.pallas import tpu as pltpu

F32 = jnp.float32
BF16 = jnp.bfloat16

EPS = 1e-6
ROPE_THETA = 10000.0
MASK_VALUE = -1e30
LOG2E = math.log2(math.e)

DIFF_HEADS = 8
DIFF_KV_HEADS = 4
DIFF_GROUP = DIFF_HEADS // DIFF_KV_HEADS
DIFF_HEAD_DIM = 64
DIFF_V_DIM = 2 * DIFF_HEAD_DIM
MLA_HEADS = 8
QK_NOPE_DIM = 128
QK_ROPE_DIM = 64
V_HEAD_DIM = 128
Q_LORA_RANK = 512
KV_LORA_RANK = 256

DIFF_Q_COLS = DIFF_HEADS * 2 * DIFF_HEAD_DIM
DIFF_K_COLS = DIFF_KV_HEADS * 2 * DIFF_HEAD_DIM
DIFF_V_COLS = DIFF_KV_HEADS * DIFF_V_DIM
MLA_KEY_COLS = KV_LORA_RANK + 2 * QK_ROPE_DIM

LANES = 128
V7X_VMEM_LIMIT = 56 * 1024 * 1024

FFN_ROWS, FFN_HIDDEN = 1024, 512
PROJ_ROWS = 256
OPROJ_ROWS = 256
DIFF_Q_TILE, MLA_Q_TILE = 256, 128
KEY_TILE = 2 * PROJ_ROWS
DECODE_PAGES = 16

_NT = (((1,), (1,)), ((), ()))


def _dot(a, b):
    return jnp.dot(a, b, preferred_element_type=F32)


def _dot_nt(a, b):
    return lax.dot_general(a, b, _NT, preferred_element_type=F32)


def _rms(x, g, axis=-1):
    return x * lax.rsqrt(jnp.mean(x * x, axis=axis, keepdims=True) + EPS) * g


def _params(sem, vmem=V7X_VMEM_LIMIT, flags=None):
    return pltpu.CompilerParams(dimension_semantics=sem, vmem_limit_bytes=vmem, flags=flags)


def _ffn_body(x_ref, pre_ref, post_ref, wg_ref, wu_ref, wd_ref, o_ref, h_ref, *, nf):
    f = pl.program_id(1)

    @pl.when(f == 0)
    def _():
        h_ref[...] = _rms(x_ref[...], pre_ref[...]).astype(BF16)
        o_ref[...] = jnp.zeros_like(o_ref)

    h = h_ref[...]
    g = _dot(h, wg_ref[0])
    u = _dot(h, wu_ref[0])
    a = (g / (1.0 + jnp.exp(-g))) * u
    o_ref[...] += _dot(a.astype(BF16), wd_ref[0])

    @pl.when(f == nf - 1)
    def _():
        o_ref[...] = x_ref[...] + 0.5 * _rms(o_ref[...], post_ref[...])


def _ffn_weights(w_gate, w_up, w_down, tf):
    d, hidden = w_gate.shape
    nf = hidden // tf
    by_chunk = lambda w: jnp.transpose(w.reshape(d, nf, tf), (1, 0, 2)).astype(BF16)
    return by_chunk(w_gate), by_chunk(w_up), w_down.reshape(nf, tf, d).astype(BF16)


def _ffn(x, pre, post, wg, wu, wd, *, tm):
    m, d = x.shape
    nf, _, tf = wg.shape
    row = lambda i, f: (i, 0)
    const = lambda i, f: (0, 0)
    chunk = lambda i, f: (f, 0, 0)
    return pl.pallas_call(
        functools.partial(_ffn_body, nf=nf),
        out_shape=jax.ShapeDtypeStruct((m, d), F32),
        grid=(m // tm, nf),
        in_specs=[pl.BlockSpec((tm, d), row, pipeline_mode=pl.Buffered(1)),
                  pl.BlockSpec((1, d), const),
                  pl.BlockSpec((1, d), const),
                  pl.BlockSpec((1, d, tf), chunk),
                  pl.BlockSpec((1, d, tf), chunk),
                  pl.BlockSpec((1, tf, d), chunk)],
        out_specs=pl.BlockSpec((tm, d), row),
        scratch_shapes=[pltpu.VMEM((tm, d), BF16)],
        compiler_params=_params(("parallel", "arbitrary")),
        name="ffn",
    )(x, pre, post, wg, wu, wd)


def _proj_body(x_ref, g_ref, win_ref, qn_ref, wuq_ref, kvn_ref, wukt_ref, cos_ref, sin_ref,
               qd_ref, kdt_ref, kdb_ref, vdf_ref, vdt_ref, ckv_ref, ckvt_ref, kpet_ref, kvm_ref, qm_ref):
    h = _rms(x_ref[...], g_ref[...]).astype(BF16)
    cos = cos_ref[...]
    sin = sin_ref[...]
    lane = lax.broadcasted_iota(jnp.int32, cos.shape, 1)
    first_half = (lane & (DIFF_HEAD_DIM - 1)) < DIFF_HEAD_DIM // 2

    def rope(z):
        swapped = jnp.where(first_half, pltpu.roll(z, LANES - 32, 1), pltpu.roll(z, 32, 1))
        return z * cos + swapped * sin

    c0 = 0
    zq = _dot(h, win_ref[:, c0:c0 + DIFF_Q_COLS])
    diff_scale = DIFF_HEAD_DIM ** -0.5 * LOG2E
    for j in range(DIFF_Q_COLS // LANES):
        sl = slice(j * LANES, (j + 1) * LANES)
        qd_ref[:, sl] = (rope(zq[:, sl]) * diff_scale).astype(BF16)
    c0 += DIFF_Q_COLS

    zk = _dot(h, win_ref[:, c0:c0 + DIFF_K_COLS])
    for j in range(DIFF_K_COLS // LANES):
        sl = slice(j * LANES, (j + 1) * LANES)
        r = rope(zk[:, sl])
        kdt_ref[0, sl, :] = r.T
        kdb_ref[:, sl] = r.astype(BF16)
    c0 += DIFF_K_COLS

    zv = _dot(h, win_ref[:, c0:c0 + DIFF_V_COLS])
    vdf_ref[...] = zv
    vdt_ref[0] = zv.T.astype(BF16)
    c0 += DIFF_V_COLS

    cq = _rms(_dot(h, win_ref[:, c0:c0 + Q_LORA_RANK]), qn_ref[...]).astype(BF16)
    c0 += Q_LORA_RANK
    qm = _dot(cq, wuq_ref[...])
    mla_scale = (QK_NOPE_DIM + QK_ROPE_DIM) ** -0.5 * LOG2E
    pe0 = MLA_HEADS * QK_NOPE_DIM
    for hh in range(MLA_HEADS):
        q_nope = qm[:, hh * QK_NOPE_DIM:(hh + 1) * QK_NOPE_DIM].astype(BF16)
        q_lat = _dot(q_nope, wukt_ref[hh])
        qm_ref[hh, :, 0:KV_LORA_RANK] = (q_lat * mla_scale).astype(BF16)
        q_pe = rope(qm[:, pe0 + hh * LANES:pe0 + (hh + 1) * LANES])
        qm_ref[hh, :, KV_LORA_RANK:MLA_KEY_COLS] = (q_pe * mla_scale).astype(BF16)

    ckv = _rms(_dot(h, win_ref[:, c0:c0 + KV_LORA_RANK]), kvn_ref[...])
    ckv_ref[...] = ckv
    kvm_ref[:, 0:KV_LORA_RANK] = ckv.astype(BF16)
    ckvt_ref[0] = ckv.T.astype(BF16)
    c0 += KV_LORA_RANK

    kpe = rope(_dot(h, win_ref[:, c0:c0 + LANES]))
    kpet_ref[0] = kpe.T[0:QK_ROPE_DIM]
    kvm_ref[:, KV_LORA_RANK:MLA_KEY_COLS] = kpe.astype(BF16)


def _proj(x, gain, w_in, q_norm, w_uq, kv_norm, w_ukt, cos, sin, *, tm, seq):
    m, d = x.shape
    nt = seq // tm
    row = lambda i: (i, 0)
    chunk = lambda i: (i, 0, 0)
    seq_cols = lambda i: (i // nt, 0, i % nt)
    const2 = lambda i: (0, 0)
    const3 = lambda i: (0, 0, 0)
    out_shape = (
        jax.ShapeDtypeStruct((m, DIFF_Q_COLS), BF16),
        jax.ShapeDtypeStruct((m // seq, DIFF_K_COLS, seq), F32),
        jax.ShapeDtypeStruct((m, DIFF_K_COLS), BF16),
        jax.ShapeDtypeStruct((m, DIFF_V_COLS), F32),
        jax.ShapeDtypeStruct((m // tm, DIFF_V_COLS, tm), BF16),
        jax.ShapeDtypeStruct((m, KV_LORA_RANK), F32),
        jax.ShapeDtypeStruct((m // tm, KV_LORA_RANK, tm), BF16),
        jax.ShapeDtypeStruct((m // seq, QK_ROPE_DIM, seq), F32),
        jax.ShapeDtypeStruct((m, MLA_KEY_COLS), BF16),
        jax.ShapeDtypeStruct((MLA_HEADS, m, MLA_KEY_COLS), BF16),
    )
    out_specs = (
        pl.BlockSpec((tm, DIFF_Q_COLS), row),
        pl.BlockSpec((1, DIFF_K_COLS, tm), seq_cols),
        pl.BlockSpec((tm, DIFF_K_COLS), row),
        pl.BlockSpec((tm, DIFF_V_COLS), row),
        pl.BlockSpec((1, DIFF_V_COLS, tm), chunk),
        pl.BlockSpec((tm, KV_LORA_RANK), row),
        pl.BlockSpec((1, KV_LORA_RANK, tm), chunk),
        pl.BlockSpec((1, QK_ROPE_DIM, tm), seq_cols),
        pl.BlockSpec((tm, MLA_KEY_COLS), row),
        pl.BlockSpec((MLA_HEADS, tm, MLA_KEY_COLS), lambda i: (0, i, 0)),
    )
    return pl.pallas_call(
        _proj_body,
        out_shape=out_shape,
        grid=(m // tm,),
        in_specs=[pl.BlockSpec((tm, d), row),
                  pl.BlockSpec((1, d), const2),
                  pl.BlockSpec(w_in.shape, const2),
                  pl.BlockSpec((1, Q_LORA_RANK), const2),
                  pl.BlockSpec(w_uq.shape, const2),
                  pl.BlockSpec((1, KV_LORA_RANK), const2),
                  pl.BlockSpec(w_ukt.shape, const3),
                  pl.BlockSpec((tm, LANES), row),
                  pl.BlockSpec((tm, LANES), row)],
        out_specs=out_specs,
        compiler_params=_params(("parallel",)),
        name="proj",
    )(x, gain, w_in, q_norm, w_uq, kv_norm, w_ukt, cos, sin)


def _diff_lambda(lq1, lk1, lq2, lk2, lam_init):
    a = jnp.exp(jnp.sum(lq1[...] * lk1[...], axis=-1, keepdims=True))
    b = jnp.exp(jnp.sum(lq2[...] * lk2[...], axis=-1, keepdims=True))
    return a - b + lam_init


def _online_update(s, m_ref, l_ref):
    m_prev = m_ref[...]
    m_new = jnp.maximum(m_prev, jnp.max(s, axis=-1, keepdims=True))
    alpha = jnp.exp2(m_prev - m_new)
    p = jnp.exp2(s - m_new)
    l_ref[...] = alpha * l_ref[...] + jnp.sum(p, axis=-1, keepdims=True)
    m_ref[...] = m_new
    return alpha, p


def _init_softmax_state(m_ref, l_ref, acc_ref):
    m_ref[...] = jnp.full_like(m_ref, MASK_VALUE)
    l_ref[...] = jnp.zeros_like(l_ref)
    acc_ref[...] = jnp.zeros_like(acc_ref)


ONES_ROWS = 16


def _flash_update_t(s, vt, ones, m_ref, acc_ref):
    m_prev = m_ref[...]
    m_new = jnp.maximum(m_prev, jnp.max(s, axis=0, keepdims=True))
    alpha = jnp.exp2(m_prev - m_new)
    p = jnp.exp2(s - m_new).astype(BF16)
    acc_ref[...] = alpha * acc_ref[...] + _dot(jnp.concatenate([vt, ones], axis=0), p)
    m_ref[...] = m_new


def _init_flash_state_t(m_ref, acc_ref):
    m_ref[...] = jnp.full_like(m_ref, MASK_VALUE)
    acc_ref[...] = jnp.zeros_like(acc_ref)


def _causal_key_loop(step, qi, tq, tk):
    def body(j, carry):
        step(j, False)
        return carry

    n_full = (qi * tq) // tk
    lax.fori_loop(0, n_full, body, 0)
    step(n_full, True)


def _causal_mask_t(shape, key_start, q_start, tq):
    key = lax.broadcasted_iota(jnp.int32, shape, 0) + key_start
    query = (lax.broadcasted_iota(jnp.int32, shape, 1) & (tq - 1)) + q_start
    return key <= query


def _diff_flash_body(q_ref, k_ref, vt_ref, lq1, lk1, lq2, lk2, subln_ref, o_ref,
                     m_ref, acc_ref, *, tq, tk, lam_init):
    qi = pl.program_id(2)
    chunks = tk // vt_ref.shape[2]
    q = q_ref[0]
    lane = lax.broadcasted_iota(jnp.int32, (tq, LANES), 1)
    comp0 = lane < DIFF_HEAD_DIM
    qa, qb = q[:, 0:LANES], q[:, LANES:2 * LANES]
    zero = jnp.zeros_like(qa)
    q4 = jnp.concatenate([jnp.where(comp0, qa, zero), jnp.where(comp0, qb, zero),
                          jnp.where(comp0, zero, qa), jnp.where(comp0, zero, qb)], axis=0)
    ones = jnp.ones((ONES_ROWS, tk), BF16)

    _init_flash_state_t(m_ref, acc_ref)

    def step(j, masked):
        start = pl.multiple_of(j * tk, tk)
        k = k_ref[0, pl.ds(start, tk), :]
        vt = jnp.concatenate([vt_ref[j * chunks + c] for c in range(chunks)], axis=1)
        s = _dot_nt(k, q4)
        if masked:
            s = jnp.where(_causal_mask_t(s.shape, start, qi * tq, tq), s, MASK_VALUE)
        _flash_update_t(s, vt, ones, m_ref, acc_ref)

    _causal_key_loop(step, qi, tq, tk)

    acc = acc_ref[...]
    o = acc[0:DIFF_V_DIM] / acc[DIFF_V_DIM:DIFF_V_DIM + 1]
    lam = _diff_lambda(lq1, lk1, lq2, lk2, lam_init)
    a = o[:, 0:2 * tq] - lam * o[:, 2 * tq:4 * tq]
    y = _rms(a, subln_ref[...], axis=0) * (1.0 - lam_init)
    o_ref[0, :, 0:LANES] = y[:, 0:tq].T.astype(BF16)
    o_ref[0, :, LANES:2 * LANES] = y[:, tq:2 * tq].T.astype(BF16)


def _diff_flash(qd, kd, vdt, lq1, lk1, lq2, lk2, subln_col, *, tq, tk, lam_init):
    b, t, _ = qd.shape
    n_chunk, _, chunk = vdt.shape
    per_batch = n_chunk // b
    const = lambda bi, ki, qi: (0, 0)
    lam_spec = pl.BlockSpec((1, DIFF_HEAD_DIM), const)
    return pl.pallas_call(
        functools.partial(_diff_flash_body, tq=tq, tk=tk, lam_init=lam_init),
        out_shape=jax.ShapeDtypeStruct((b, t, DIFF_HEADS * DIFF_V_DIM), BF16),
        grid=(b, DIFF_KV_HEADS, t // tq),
        in_specs=[pl.BlockSpec((1, tq, 2 * LANES), lambda bi, ki, qi: (bi, qi, ki)),
                  pl.BlockSpec((1, t, LANES), lambda bi, ki, qi: (bi, 0, ki)),
                  pl.BlockSpec((per_batch, DIFF_V_DIM, chunk), lambda bi, ki, qi: (bi, ki, 0)),
                  lam_spec, lam_spec, lam_spec, lam_spec,
                  pl.BlockSpec((DIFF_V_DIM, 1), const)],
        out_specs=pl.BlockSpec((1, tq, 2 * LANES), lambda bi, ki, qi: (bi, qi, ki)),
        scratch_shapes=[pltpu.VMEM((1, 4 * tq), F32),
                        pltpu.VMEM((DIFF_V_DIM + ONES_ROWS, 4 * tq), F32)],
        compiler_params=_params(("parallel", "parallel", "arbitrary")),
        name="diff_flash",
    )(qd, kd, vdt, lq1, lk1, lq2, lk2, subln_col)


def _mla_flash_body(q_ref, kv_ref, ckvt_ref, wuvt_ref, o_ref, m_ref, acc_ref, *, tq, tk):
    qi = pl.program_id(1)
    chunks = tk // ckvt_ref.shape[2]
    q = jnp.concatenate([q_ref[hh] for hh in range(MLA_HEADS)], axis=0)
    ones = jnp.ones((ONES_ROWS, tk), BF16)

    _init_flash_state_t(m_ref, acc_ref)

    def step(j, masked):
        start = pl.multiple_of(j * tk, tk)
        kv = kv_ref[0, pl.ds(start, tk), :]
        ct = jnp.concatenate([ckvt_ref[j * chunks + c] for c in range(chunks)], axis=1)
        s = _dot_nt(kv, q)
        if masked:
            s = jnp.where(_causal_mask_t(s.shape, start, qi * tq, tq), s, MASK_VALUE)
        _flash_update_t(s, ct, ones, m_ref, acc_ref)

    _causal_key_loop(step, qi, tq, tk)

    acc = acc_ref[...]
    o = (acc[0:KV_LORA_RANK] / acc[KV_LORA_RANK:KV_LORA_RANK + 1]).astype(BF16)
    for hh in range(MLA_HEADS):
        out_t = _dot(wuvt_ref[hh], o[:, hh * tq:(hh + 1) * tq])
        o_ref[0, :, hh * V_HEAD_DIM:(hh + 1) * V_HEAD_DIM] = out_t.T.astype(BF16)


def _mla_flash(qm, kvm, ckvt, w_uvt, *, tq, tk):
    b, t, _ = kvm.shape
    n_chunk, _, chunk = ckvt.shape
    per_batch = n_chunk // b
    nq = t // tq
    return pl.pallas_call(
        functools.partial(_mla_flash_body, tq=tq, tk=tk),
        out_shape=jax.ShapeDtypeStruct((b, t, MLA_HEADS * V_HEAD_DIM), BF16),
        grid=(b, nq),
        in_specs=[pl.BlockSpec((MLA_HEADS, tq, MLA_KEY_COLS), lambda bi, qi: (0, bi * nq + qi, 0)),
                  pl.BlockSpec((1, t, MLA_KEY_COLS), lambda bi, qi: (bi, 0, 0)),
                  pl.BlockSpec((per_batch, KV_LORA_RANK, chunk), lambda bi, qi: (bi, 0, 0)),
                  pl.BlockSpec(w_uvt.shape, lambda bi, qi: (0, 0, 0))],
        out_specs=pl.BlockSpec((1, tq, MLA_HEADS * V_HEAD_DIM), lambda bi, qi: (bi, qi, 0)),
        scratch_shapes=[pltpu.VMEM((1, MLA_HEADS * tq), F32),
                        pltpu.VMEM((KV_LORA_RANK + ONES_ROWS, MLA_HEADS * tq), F32)],
        compiler_params=_params(("parallel", "arbitrary")),
        name="mla_flash",
    )(qm, kvm, ckvt, w_uvt)


def _decode_body(pt_ref, qbd_ref, qm_ref, knew_ref, vnew_ref, cnew_ref, pnew_ref,
                 lq1, lk1, lq2, lk2, subln_ref, ck_hbm, cv_hbm, cc_hbm, cp_hbm, od_ref, om_ref,
                 kbuf, vbuf, cbuf, pbuf, sems, md_ref, ld_ref, accd_ref, mm_ref, lm_ref, accm_ref,
                 *, pages, lam_init):
    b = pl.program_id(0)
    step = pl.program_id(1)
    n_steps = pl.num_programs(1)
    flat = b * n_steps + step
    slot = flat & 1
    n_rows_d = 2 * DIFF_HEADS
    page = cbuf.shape[2]
    caches = ((ck_hbm, kbuf), (cv_hbm, vbuf), (cc_hbm, cbuf), (cp_hbm, pbuf))

    def page_copy(cache, page_idx, slot_, i):
        hbm, buf = caches[cache]
        return pltpu.make_async_copy(hbm.at[page_idx], buf.at[slot_, i], sems.at[slot_, cache])

    def start_fetch(b_, step_, slot_):
        for i in range(pages):
            page_idx = pt_ref[b_, step_ * pages + i]
            for cache in range(len(caches)):
                page_copy(cache, page_idx, slot_, i).start()

    def wait_fetch(slot_):
        for i in range(pages):
            for cache in range(len(caches)):
                page_copy(cache, 0, slot_, i).wait()

    @pl.when(flat == 0)
    def _():
        start_fetch(0, 0, 0)

    @pl.when(flat + 1 < pl.num_programs(0) * n_steps)
    def _():
        wraps = step + 1 == n_steps
        start_fetch(jnp.where(wraps, b + 1, b), jnp.where(wraps, 0, step + 1), 1 - slot)

    wait_fetch(slot)
    k_refs = [kbuf.at[slot, i] for i in range(pages)]
    v_refs = [vbuf.at[slot, i] for i in range(pages)]
    c_refs = [cbuf.at[slot, i] for i in range(pages)]
    p_refs = [pbuf.at[slot, i] for i in range(pages)]

    @pl.when(step == 0)
    def _():
        _init_softmax_state(md_ref, ld_ref, accd_ref)
        _init_softmax_state(mm_ref, lm_ref, accm_ref)

    qbd = qbd_ref[0]
    qm = qm_ref[0]
    q_lat = qm[:, 0:KV_LORA_RANK]
    q_pe = qm[:, KV_LORA_RANK:KV_LORA_RANK + QK_ROPE_DIM]
    row_head = (lax.broadcasted_iota(jnp.int32, (n_rows_d, DIFF_V_DIM), 0) & (DIFF_HEADS - 1)) // DIFF_GROUP

    sd = jnp.concatenate([_dot(qbd, k_refs[i][...].astype(BF16)) for i in range(pages)], axis=1)
    alpha, p = _online_update(sd, md_ref, ld_ref)
    p = p.astype(BF16)
    pv = jnp.zeros((n_rows_d, DIFF_V_DIM), F32)
    for kk in range(DIFF_KV_HEADS):
        pv_k = _dot(p[:, 0:page], v_refs[0][pl.ds(kk, page, stride=DIFF_KV_HEADS), :].astype(BF16))
        for i in range(1, pages):
            pv_k += _dot(p[:, i * page:(i + 1) * page],
                         v_refs[i][pl.ds(kk, page, stride=DIFF_KV_HEADS), :].astype(BF16))
        pv += jnp.where(row_head == kk, pv_k, 0.0)
    accd_ref[...] = alpha * accd_ref[...] + pv

    cs = [c_refs[i][...].astype(BF16) for i in range(pages)]
    sm = jnp.concatenate([_dot_nt(q_lat, cs[i]) + _dot(q_pe, p_refs[i][...].astype(BF16))
                          for i in range(pages)], axis=1)
    alpha, p = _online_update(sm, mm_ref, lm_ref)
    p = p.astype(BF16)
    pv = _dot(p[:, 0:page], cs[0])
    for i in range(1, pages):
        pv += _dot(p[:, i * page:(i + 1) * page], cs[i])
    accm_ref[...] = alpha * accm_ref[...] + pv

    @pl.when(step == pl.num_programs(1) - 1)
    def _():
        round_bf16 = lambda a: a.astype(BF16).astype(F32)
        k_new = round_bf16(knew_ref[0])
        v_new = round_bf16(vnew_ref[0])
        s_new = jnp.sum(qbd.astype(F32) * k_new, axis=-1, keepdims=True)
        alpha, p_new = _online_update(s_new, md_ref, ld_ref)
        v_own = jnp.zeros((n_rows_d, DIFF_V_DIM), F32)
        for kk in range(DIFF_KV_HEADS):
            v_own += jnp.where(row_head == kk, v_new[:, kk * DIFF_V_DIM:(kk + 1) * DIFF_V_DIM], 0.0)
        o = (alpha * accd_ref[...] + round_bf16(p_new) * v_own) / ld_ref[...]
        lam = _diff_lambda(lq1, lk1, lq2, lk2, lam_init)
        a = o[0:DIFF_HEADS] - lam * o[DIFF_HEADS:n_rows_d]
        od_ref[0] = _rms(a, subln_ref[...]) * (1.0 - lam_init)

        c_new = round_bf16(cnew_ref[0])
        pe_new = round_bf16(pnew_ref[0])
        s_new = (jnp.sum(q_lat.astype(F32) * c_new, axis=-1, keepdims=True)
                 + jnp.sum(q_pe.astype(F32) * pe_new, axis=-1, keepdims=True))
        alpha, p_new = _online_update(s_new, mm_ref, lm_ref)
        om_ref[0] = (alpha * accm_ref[...] + round_bf16(p_new) * c_new) / lm_ref[...]


def _decode(page_table, qbd, qm, k_new, v_new, c_new, pe_new, lq1, lk1, lq2, lk2, subln,
            cache_kt, cache_v, cache_c, cache_pt, *, pages, lam_init):
    nb, n_pages = page_table.shape
    per_sample = lambda b, s, pt: (b, 0, 0)
    const = lambda b, s, pt: (0, 0)
    lam_spec = pl.BlockSpec((1, DIFF_HEAD_DIM), const)
    caches = (cache_kt, cache_v, cache_c, cache_pt)
    n_rows_d = 2 * DIFF_HEADS
    grid_spec = pltpu.PrefetchScalarGridSpec(
        num_scalar_prefetch=1,
        grid=(nb, n_pages // pages),
        in_specs=[pl.BlockSpec((1, n_rows_d, DIFF_K_COLS), per_sample),
                  pl.BlockSpec((1, MLA_HEADS, MLA_KEY_COLS), per_sample),
                  pl.BlockSpec((1, 1, DIFF_K_COLS), per_sample),
                  pl.BlockSpec((1, 1, DIFF_V_COLS), per_sample),
                  pl.BlockSpec((1, 1, KV_LORA_RANK), per_sample),
                  pl.BlockSpec((1, 1, QK_ROPE_DIM), per_sample),
                  lam_spec, lam_spec, lam_spec, lam_spec,
                  pl.BlockSpec((1, DIFF_V_DIM), const)]
                 + [pl.BlockSpec(memory_space=pl.ANY) for _ in caches],
        out_specs=(pl.BlockSpec((1, DIFF_HEADS, DIFF_V_DIM), per_sample),
                   pl.BlockSpec((1, MLA_HEADS, KV_LORA_RANK), per_sample)),
        scratch_shapes=[pltpu.VMEM((2, pages) + c.shape[1:], c.dtype) for c in caches]
                       + [pltpu.SemaphoreType.DMA((2, len(caches))),
                          pltpu.VMEM((n_rows_d, 1), F32), pltpu.VMEM((n_rows_d, 1), F32),
                        pltpu.VMEM((n_rows_d, DIFF_V_DIM), F32),
                        pltpu.VMEM((MLA_HEADS, 1), F32), pltpu.VMEM((MLA_HEADS, 1), F32),
                        pltpu.VMEM((MLA_HEADS, KV_LORA_RANK), F32)],
    )
    return pl.pallas_call(
        functools.partial(_decode_body, pages=pages, lam_init=lam_init),
        out_shape=(jax.ShapeDtypeStruct((nb, DIFF_HEADS, DIFF_V_DIM), F32),
                   jax.ShapeDtypeStruct((nb, MLA_HEADS, KV_LORA_RANK), F32)),
        grid_spec=grid_spec,
        compiler_params=_params(("arbitrary", "arbitrary")),
        name="decode",
    )(page_table, qbd, qm, k_new, v_new, c_new, pe_new, lq1, lk1, lq2, lk2, subln, *caches)


def _uv_body(o_ref, wuv_ref, out_ref):
    for hh in range(MLA_HEADS):
        out_ref[:, hh * V_HEAD_DIM:(hh + 1) * V_HEAD_DIM] = _dot(
            o_ref[hh].astype(BF16), wuv_ref[hh]).astype(BF16)


def _uv(o_lat, w_uvh):
    _, m, _ = o_lat.shape
    return pl.pallas_call(
        _uv_body,
        out_shape=jax.ShapeDtypeStruct((m, MLA_HEADS * V_HEAD_DIM), BF16),
        name="uv",
    )(o_lat, w_uvh)


def _oproj_body(md_ref, mm_ref, wo_ref, x_ref, g_ref, o_ref):
    half = md_ref.shape[1]
    y = _dot(md_ref[...], wo_ref[0:half, :]) + _dot(mm_ref[...], wo_ref[half:2 * half, :])
    o_ref[...] = x_ref[...] + _rms(y, g_ref[...])


def _oproj(mixed_d, mixed_m, w_o, x, gain, *, tm):
    m, d = x.shape
    half = mixed_d.shape[1]
    row = lambda i: (i, 0)
    const = lambda i: (0, 0)
    return pl.pallas_call(
        _oproj_body,
        out_shape=jax.ShapeDtypeStruct((m, d), F32),
        grid=(m // tm,),
        in_specs=[pl.BlockSpec((tm, half), row),
                  pl.BlockSpec((tm, half), row),
                  pl.BlockSpec(w_o.shape, const),
                  pl.BlockSpec((tm, d), row),
                  pl.BlockSpec((1, d), const)],
        out_specs=pl.BlockSpec((tm, d), row),
        compiler_params=_params(("parallel",)),
        name="oproj",
    )(mixed_d, mixed_m, w_o, x, gain)


def _rope_tables(pos):
    half = DIFF_HEAD_DIM // 2
    inv = jnp.exp(-math.log(ROPE_THETA) * jnp.arange(half, dtype=F32) * (2.0 / DIFF_HEAD_DIM))
    ang = pos[:, None] * inv[None, :]
    cos, sin = jnp.cos(ang), jnp.sin(ang)
    return jnp.tile(cos, (1, 4)), jnp.concatenate([-sin, sin, -sin, sin], axis=1)


def kernel(x_prompt, x_sample, cache_diff_k, cache_diff_v, cache_mla_ckv, cache_mla_kpe, page_table, ln_ffn1_pre, ln_ffn1_post, ffn1_w_gate, ffn1_w_up, ffn1_w_down, ln_mix_pre, ln_mix_post, w_in, diff_lambda_q1, diff_lambda_k1, diff_lambda_q2, diff_lambda_k2, diff_subln, mla_q_norm, w_uq, mla_kv_norm, w_uk, w_uv, w_o, ln_ffn2_pre, ln_ffn2_post, ffn2_w_gate, ffn2_w_up, ffn2_w_down):
    bsz, t_p, d = x_prompt.shape
    nb, t_s, _ = x_sample.shape
    assert t_s == 1, "one new token per sample"
    depth = w_in.shape[0]
    n_phys, page = cache_diff_k.shape[1:3]
    past_len = page_table.shape[1] * page

    xp = x_prompt.reshape(bsz * t_p, d)
    xs = x_sample.reshape(nb * t_s, d)
    cos_p, sin_p = _rope_tables(jnp.tile(jnp.arange(t_p, dtype=F32), bsz))
    cos_s, sin_s = _rope_tables(jnp.full((nb,), past_len, dtype=F32))

    new_p = [[], [], [], []]
    new_s = [[], [], [], []]
    for l in range(depth):
        lam_init = 0.8 - 0.6 * math.exp(-0.3 * l)
        bf = lambda w: w.astype(BF16)
        ffn1 = (ln_ffn1_pre[l][None], ln_ffn1_post[l][None],
                *_ffn_weights(ffn1_w_gate[l], ffn1_w_up[l], ffn1_w_down[l], FFN_HIDDEN))
        ffn2 = (ln_ffn2_pre[l][None], ln_ffn2_post[l][None],
                *_ffn_weights(ffn2_w_gate[l], ffn2_w_up[l], ffn2_w_down[l], FFN_HIDDEN))
        w_in_p = bf(jnp.pad(w_in[l], ((0, 0), (0, LANES - QK_ROPE_DIM))))
        w_uq_h = w_uq[l].reshape(Q_LORA_RANK, MLA_HEADS, QK_NOPE_DIM + QK_ROPE_DIM)
        w_uq_p = bf(jnp.concatenate([
            w_uq_h[:, :, :QK_NOPE_DIM].reshape(Q_LORA_RANK, MLA_HEADS * QK_NOPE_DIM),
            jnp.pad(w_uq_h[:, :, QK_NOPE_DIM:], ((0, 0), (0, 0), (0, LANES - QK_ROPE_DIM))
                    ).reshape(Q_LORA_RANK, MLA_HEADS * LANES)], axis=1))
        w_ukt = bf(jnp.transpose(w_uk[l], (1, 2, 0)))
        w_uvh = bf(jnp.transpose(w_uv[l], (1, 0, 2)))
        w_uvt = bf(jnp.transpose(w_uv[l], (1, 2, 0)))
        w_o_b = bf(w_o[l])
        proj_w = (ln_mix_pre[l][None], w_in_p, mla_q_norm[l][None], w_uq_p, mla_kv_norm[l][None], w_ukt)
        lams = (diff_lambda_q1[l][None], diff_lambda_k1[l][None], diff_lambda_q2[l][None], diff_lambda_k2[l][None])
        subln = diff_subln[l][None]

        xp = _ffn(xp, *ffn1, tm=FFN_ROWS)
        xs = _ffn(xs, *ffn1, tm=nb)

        qd, kdt, kdb, vdf, vdt, ckv, ckvt, kpet, kvm, qm = _proj(xp, *proj_w, cos_p, sin_p, tm=PROJ_ROWS, seq=t_p)
        mixed_d = _diff_flash(qd.reshape(bsz, t_p, -1), kdb.reshape(bsz, t_p, -1), vdt, *lams,
                              diff_subln[l][:, None], tq=DIFF_Q_TILE, tk=KEY_TILE, lam_init=lam_init)
        mixed_m = _mla_flash(qm, kvm.reshape(bsz, t_p, -1), ckvt, w_uvt, tq=MLA_Q_TILE, tk=KEY_TILE)
        xp = _oproj(mixed_d.reshape(bsz * t_p, -1), mixed_m.reshape(bsz * t_p, -1), w_o_b, xp,
                    ln_mix_post[l][None], tm=OPROJ_ROWS)
        kd_p = jnp.transpose(kdt.reshape(bsz, DIFF_KV_HEADS, 2, DIFF_HEAD_DIM, t_p), (0, 4, 1, 2, 3))
        for acc, val in zip(new_p, (kd_p, vdf, ckv, jnp.transpose(kpet, (0, 2, 1)))):
            acc.append(val)

        qd, kdt, _, vdf, _, ckv, _, kpet, _, qm = _proj(xs, *proj_w, cos_s, sin_s, tm=nb, seq=nb)
        kdf, kpe = kdt[0].T, kpet[0].T
        q5 = qd.astype(F32).reshape(nb, DIFF_KV_HEADS, DIFF_GROUP, 2, DIFF_HEAD_DIM)
        qbd = jnp.einsum('bkgcd,kK,cC->bckgKCd', q5, jnp.eye(DIFF_KV_HEADS, dtype=F32), jnp.eye(2, dtype=F32))
        qbd = qbd.reshape(nb, 2 * DIFF_HEADS, DIFF_K_COLS).astype(BF16)
        cache_kt = jnp.transpose(cache_diff_k[l].reshape(n_phys, page, DIFF_K_COLS), (0, 2, 1))
        cache_v = cache_diff_v[l].reshape(n_phys, page * DIFF_KV_HEADS, DIFF_V_DIM)
        cache_pt = jnp.transpose(cache_mla_kpe[l], (0, 2, 1))
        o_d, o_lat = _decode(
            page_table, qbd, jnp.transpose(qm, (1, 0, 2)),
            kdf[:, None], vdf[:, None], ckv[:, None], kpe[:, None], *lams, subln,
            cache_kt, cache_v, cache_mla_ckv[l], cache_pt, pages=DECODE_PAGES, lam_init=lam_init)
        mixed_d = o_d.reshape(nb, DIFF_HEADS * DIFF_V_DIM).astype(BF16)
        mixed_m = _uv(jnp.transpose(o_lat, (1, 0, 2)), w_uvh)
        xs = _oproj(mixed_d, mixed_m, w_o_b, xs, ln_mix_post[l][None], tm=nb)
        for acc, val in zip(new_s, (kdf, vdf, ckv, kpe)):
            acc.append(val)

        xp = _ffn(xp, *ffn2, tm=FFN_ROWS)
        xs = _ffn(xs, *ffn2, tm=nb)

    kd_shape = (DIFF_KV_HEADS, 2, DIFF_HEAD_DIM)
    vd_shape = (DIFF_KV_HEADS, DIFF_V_DIM)
    stack = lambda vals, lead, tail: jnp.stack([v.reshape(*lead, *tail) for v in vals])
    lead_p, lead_s = (bsz, t_p), (nb, t_s)
    return (xp.reshape(bsz, t_p, d), xs.reshape(nb, t_s, d),
            stack(new_p[0], lead_p, kd_shape), stack(new_p[1], lead_p, vd_shape),
            stack(new_p[2], lead_p, (KV_LORA_RANK,)), stack(new_p[3], lead_p, (QK_ROPE_DIM,)),
            stack(new_s[0], lead_s, kd_shape), stack(new_s[1], lead_s, vd_shape),
            stack(new_s[2], lead_s, (KV_LORA_RANK,)), stack(new_s[3], lead_s, (QK_ROPE_DIM,)))
```

```python
import functools
import math

import jax
import jax.numpy as jnp
from jax import lax
from jax.experimental import pallas as pl
from jax.experimental.pallas import tpu as pltpu

F32 = jnp.float32
BF16 = jnp.bfloat16

EPS = 1e-6
ROPE_THETA = 10000.0
MASK_VALUE = -1e30
LOG2E = math.log2(math.e)

DIFF_HEADS = 8
DIFF_KV_HEADS = 4
DIFF_GROUP = DIFF_HEADS // DIFF_KV_HEADS
DIFF_HEAD_DIM = 64
DIFF_V_DIM = 2 * DIFF_HEAD_DIM
MLA_HEADS = 8
QK_NOPE_DIM = 128
QK_ROPE_DIM = 64
V_HEAD_DIM = 128
Q_LORA_RANK = 512
KV_LORA_RANK = 256

DIFF_Q_COLS = DIFF_HEADS * 2 * DIFF_HEAD_DIM
DIFF_K_COLS = DIFF_KV_HEADS * 2 * DIFF_HEAD_DIM
DIFF_V_COLS = DIFF_KV_HEADS * DIFF_V_DIM
MLA_KEY_COLS = KV_LORA_RANK + 2 * QK_ROPE_DIM

LANES = 128
V7X_VMEM_LIMIT = 56 * 1024 * 1024

FFN_ROWS, FFN_HIDDEN = 1024, 256
PROJ_ROWS = 256
OPROJ_ROWS = 256
DIFF_Q_TILE, MLA_Q_TILE = 256, 128
KEY_TILE = 2 * PROJ_ROWS
DECODE_PAGES = 16

_NT = (((1,), (1,)), ((), ()))


def _dot(a, b):
    return jnp.dot(a, b, preferred_element_type=F32)


def _dot_nt(a, b):
    return lax.dot_general(a, b, _NT, preferred_element_type=F32)


def _rms(x, g, axis=-1):
    return x * lax.rsqrt(jnp.mean(x * x, axis=axis, keepdims=True) + EPS) * g


def _params(sem, vmem=V7X_VMEM_LIMIT, flags=None):
    return pltpu.CompilerParams(dimension_semantics=sem, vmem_limit_bytes=vmem, flags=flags)


def _ffn_body(x_ref, pre_ref, post_ref, wg_ref, wu_ref, wd_ref, o_ref, h_ref, *, nf):
    f = pl.program_id(1)

    @pl.when(f == 0)
    def _():
        h_ref[...] = _rms(x_ref[...], pre_ref[...]).astype(BF16)
        o_ref[...] = jnp.zeros_like(o_ref)

    h = h_ref[...]
    g = _dot(h, wg_ref[...].astype(BF16))
    u = _dot(h, wu_ref[...].astype(BF16))
    a = (g / (1.0 + jnp.exp(-g))) * u
    o_ref[...] += _dot(a.astype(BF16), wd_ref[...].astype(BF16))

    @pl.when(f == nf - 1)
    def _():
        o_ref[...] = x_ref[...] + 0.5 * _rms(o_ref[...], post_ref[...])


def _ffn(x, pre, post, wg, wu, wd, *, tm, tf):
    m, d = x.shape
    nf = wg.shape[1] // tf
    row = lambda i, f: (i, 0)
    const = lambda i, f: (0, 0)
    return pl.pallas_call(
        functools.partial(_ffn_body, nf=nf),
        out_shape=jax.ShapeDtypeStruct((m, d), F32),
        grid=(m // tm, nf),
        in_specs=[pl.BlockSpec((tm, d), row, pipeline_mode=pl.Buffered(1)),
                  pl.BlockSpec((1, d), const),
                  pl.BlockSpec((1, d), const),
                  pl.BlockSpec((d, tf), lambda i, f: (0, f)),
                  pl.BlockSpec((d, tf), lambda i, f: (0, f)),
                  pl.BlockSpec((tf, d), lambda i, f: (f, 0))],
        out_specs=pl.BlockSpec((tm, d), row),
        scratch_shapes=[pltpu.VMEM((tm, d), BF16)],
        compiler_params=_params(("parallel", "arbitrary")),
        name="ffn",
    )(x, pre, post, wg, wu, wd)


def _proj_body(x_ref, g_ref, win_ref, qn_ref, wuq_ref, kvn_ref, wukt_ref, cos_ref, sin_ref,
               qd_ref, kdt_ref, kdb_ref, vdf_ref, vdt_ref, ckv_ref, ckvt_ref, kpet_ref, kvm_ref, qm_ref):
    h = _rms(x_ref[...], g_ref[...]).astype(BF16)
    cos = cos_ref[...]
    sin = sin_ref[...]
    lane = lax.broadcasted_iota(jnp.int32, cos.shape, 1)
    first_half = (lane & (DIFF_HEAD_DIM - 1)) < DIFF_HEAD_DIM // 2

    def rope(z):
        swapped = jnp.where(first_half, pltpu.roll(z, LANES - 32, 1), pltpu.roll(z, 32, 1))
        return z * cos + swapped * sin

    c0 = 0
    zq = _dot(h, win_ref[:, c0:c0 + DIFF_Q_COLS])
    diff_scale = DIFF_HEAD_DIM ** -0.5 * LOG2E
    for j in range(DIFF_Q_COLS // LANES):
        sl = slice(j * LANES, (j + 1) * LANES)
        qd_ref[:, sl] = (rope(zq[:, sl]) * diff_scale).astype(BF16)
    c0 += DIFF_Q_COLS

    zk = _dot(h, win_ref[:, c0:c0 + DIFF_K_COLS])
    for j in range(DIFF_K_COLS // LANES):
        sl = slice(j * LANES, (j + 1) * LANES)
        r = rope(zk[:, sl])
        kdt_ref[0, sl, :] = r.T
        kdb_ref[:, sl] = r.astype(BF16)
    c0 += DIFF_K_COLS

    zv = _dot(h, win_ref[:, c0:c0 + DIFF_V_COLS])
    vdf_ref[...] = zv
    vdt_ref[0] = zv.T.astype(BF16)
    c0 += DIFF_V_COLS

    cq = _rms(_dot(h, win_ref[:, c0:c0 + Q_LORA_RANK]), qn_ref[...]).astype(BF16)
    c0 += Q_LORA_RANK
    qm = _dot(cq, wuq_ref[...])
    mla_scale = (QK_NOPE_DIM + QK_ROPE_DIM) ** -0.5 * LOG2E
    pe0 = MLA_HEADS * QK_NOPE_DIM
    for hh in range(MLA_HEADS):
        q_nope = qm[:, hh * QK_NOPE_DIM:(hh + 1) * QK_NOPE_DIM].astype(BF16)
        q_lat = _dot(q_nope, wukt_ref[hh])
        qm_ref[hh, :, 0:KV_LORA_RANK] = (q_lat * mla_scale).astype(BF16)
        q_pe = rope(qm[:, pe0 + hh * LANES:pe0 + (hh + 1) * LANES])
        qm_ref[hh, :, KV_LORA_RANK:MLA_KEY_COLS] = (q_pe * mla_scale).astype(BF16)

    ckv = _rms(_dot(h, win_ref[:, c0:c0 + KV_LORA_RANK]), kvn_ref[...])
    ckv_ref[...] = ckv
    kvm_ref[:, 0:KV_LORA_RANK] = ckv.astype(BF16)
    ckvt_ref[0] = ckv.T.astype(BF16)
    c0 += KV_LORA_RANK

    kpe = rope(_dot(h, win_ref[:, c0:c0 + LANES]))
    kpet_ref[0] = kpe.T[0:QK_ROPE_DIM]
    kvm_ref[:, KV_LORA_RANK:MLA_KEY_COLS] = kpe.astype(BF16)


def _proj(x, gain, w_in, q_norm, w_uq, kv_norm, w_ukt, cos, sin, *, tm, seq):
    m, d = x.shape
    nt = seq // tm
    row = lambda i: (i, 0)
    chunk = lambda i: (i, 0, 0)
    seq_cols = lambda i: (i // nt, 0, i % nt)
    const2 = lambda i: (0, 0)
    const3 = lambda i: (0, 0, 0)
    out_shape = (
        jax.ShapeDtypeStruct((m, DIFF_Q_COLS), BF16),
        jax.ShapeDtypeStruct((m // seq, DIFF_K_COLS, seq), F32),
        jax.ShapeDtypeStruct((m, DIFF_K_COLS), BF16),
        jax.ShapeDtypeStruct((m, DIFF_V_COLS), F32),
        jax.ShapeDtypeStruct((m // tm, DIFF_V_COLS, tm), BF16),
        jax.ShapeDtypeStruct((m, KV_LORA_RANK), F32),
        jax.ShapeDtypeStruct((m // tm, KV_LORA_RANK, tm), BF16),
        jax.ShapeDtypeStruct((m // seq, QK_ROPE_DIM, seq), F32),
        jax.ShapeDtypeStruct((m, MLA_KEY_COLS), BF16),
        jax.ShapeDtypeStruct((MLA_HEADS, m, MLA_KEY_COLS), BF16),
    )
    out_specs = (
        pl.BlockSpec((tm, DIFF_Q_COLS), row),
        pl.BlockSpec((1, DIFF_K_COLS, tm), seq_cols),
        pl.BlockSpec((tm, DIFF_K_COLS), row),
        pl.BlockSpec((tm, DIFF_V_COLS), row),
        pl.BlockSpec((1, DIFF_V_COLS, tm), chunk),
        pl.BlockSpec((tm, KV_LORA_RANK), row),
        pl.BlockSpec((1, KV_LORA_RANK, tm), chunk),
        pl.BlockSpec((1, QK_ROPE_DIM, tm), seq_cols),
        pl.BlockSpec((tm, MLA_KEY_COLS), row),
        pl.BlockSpec((MLA_HEADS, tm, MLA_KEY_COLS), lambda i: (0, i, 0)),
    )
    return pl.pallas_call(
        _proj_body,
        out_shape=out_shape,
        grid=(m // tm,),
        in_specs=[pl.BlockSpec((tm, d), row),
                  pl.BlockSpec((1, d), const2),
                  pl.BlockSpec(w_in.shape, const2),
                  pl.BlockSpec((1, Q_LORA_RANK), const2),
                  pl.BlockSpec(w_uq.shape, const2),
                  pl.BlockSpec((1, KV_LORA_RANK), const2),
                  pl.BlockSpec(w_ukt.shape, const3),
                  pl.BlockSpec((tm, LANES), row),
                  pl.BlockSpec((tm, LANES), row)],
        out_specs=out_specs,
        compiler_params=_params(("parallel",)),
        name="proj",
    )(x, gain, w_in, q_norm, w_uq, kv_norm, w_ukt, cos, sin)


def _diff_lambda(lq1, lk1, lq2, lk2, lam_init):
    a = jnp.exp(jnp.sum(lq1[...] * lk1[...], axis=-1, keepdims=True))
    b = jnp.exp(jnp.sum(lq2[...] * lk2[...], axis=-1, keepdims=True))
    return a - b + lam_init


def _online_update(s, m_ref, l_ref):
    m_prev = m_ref[...]
    m_new = jnp.maximum(m_prev, jnp.max(s, axis=-1, keepdims=True))
    alpha = jnp.exp2(m_prev - m_new)
    p = jnp.exp2(s - m_new)
    l_ref[...] = alpha * l_ref[...] + jnp.sum(p, axis=-1, keepdims=True)
    m_ref[...] = m_new
    return alpha, p


def _init_softmax_state(m_ref, l_ref, acc_ref):
    m_ref[...] = jnp.full_like(m_ref, MASK_VALUE)
    l_ref[...] = jnp.zeros_like(l_ref)
    acc_ref[...] = jnp.zeros_like(acc_ref)


ONES_ROWS = 16


def _flash_update_t(s, vt, ones, m_ref, acc_ref):
    m_prev = m_ref[...]
    m_new = jnp.maximum(m_prev, jnp.max(s, axis=0, keepdims=True))
    alpha = jnp.exp2(m_prev - m_new)
    p = jnp.exp2(s - m_new).astype(BF16)
    acc_ref[...] = alpha * acc_ref[...] + _dot(jnp.concatenate([vt, ones], axis=0), p)
    m_ref[...] = m_new


def _init_flash_state_t(m_ref, acc_ref):
    m_ref[...] = jnp.full_like(m_ref, MASK_VALUE)
    acc_ref[...] = jnp.zeros_like(acc_ref)


def _causal_key_loop(step, qi, tq, tk):
    def body(j, carry):
        step(j, False)
        return carry

    n_full = (qi * tq) // tk
    lax.fori_loop(0, n_full, body, 0)
    step(n_full, True)


def _causal_mask_t(shape, key_start, q_start, tq):
    key = lax.broadcasted_iota(jnp.int32, shape, 0) + key_start
    query = (lax.broadcasted_iota(jnp.int32, shape, 1) & (tq - 1)) + q_start
    return key <= query


def _diff_flash_body(q_ref, k_ref, vt_ref, lq1, lk1, lq2, lk2, subln_ref, o_ref,
                     m_ref, acc_ref, s_ref, *, tq, tk, lam_init):
    qi = pl.program_id(2)
    chunks = tk // vt_ref.shape[2]
    q = q_ref[0]
    lane = lax.broadcasted_iota(jnp.int32, (tq, LANES), 1)
    comp0 = lane < DIFF_HEAD_DIM
    qa, qb = q[:, 0:LANES], q[:, LANES:2 * LANES]
    zero = jnp.zeros_like(qa)
    q4 = jnp.concatenate([jnp.where(comp0, qa, zero), jnp.where(comp0, qb, zero),
                          jnp.where(comp0, zero, qa), jnp.where(comp0, zero, qb)], axis=0)
    ones = jnp.ones((ONES_ROWS, tk), BF16)

    _init_flash_state_t(m_ref, acc_ref)

    def step(j, masked):
        start = pl.multiple_of(j * tk, tk)
        vt = jnp.concatenate([vt_ref[j * chunks + c] for c in range(chunks)], axis=1)
        s = s_ref[j & 1]
        if masked:
            s = jnp.where(_causal_mask_t(s.shape, start, qi * tq, tq), s, MASK_VALUE)
        else:
            nxt = pl.multiple_of((j + 1) * tk, tk)
            s_ref[(j + 1) & 1] = _dot_nt(k_ref[0, pl.ds(nxt, tk), :], q4)
        _flash_update_t(s, vt, ones, m_ref, acc_ref)

    s_ref[0] = _dot_nt(k_ref[0, pl.ds(0, tk), :], q4)
    _causal_key_loop(step, qi, tq, tk)

    acc = acc_ref[...]
    o = acc[0:DIFF_V_DIM] / acc[DIFF_V_DIM:DIFF_V_DIM + 1]
    lam = _diff_lambda(lq1, lk1, lq2, lk2, lam_init)
    a = o[:, 0:2 * tq] - lam * o[:, 2 * tq:4 * tq]
    y = _rms(a, subln_ref[...], axis=0) * (1.0 - lam_init)
    o_ref[0, :, 0:LANES] = y[:, 0:tq].T.astype(BF16)
    o_ref[0, :, LANES:2 * LANES] = y[:, tq:2 * tq].T.astype(BF16)


def _diff_flash(qd, kd, vdt, lq1, lk1, lq2, lk2, subln_col, *, tq, tk, lam_init):
    b, t, _ = qd.shape
    n_chunk, _, chunk = vdt.shape
    per_batch = n_chunk // b
    const = lambda bi, ki, qi: (0, 0)
    lam_spec = pl.BlockSpec((1, DIFF_HEAD_DIM), const)
    return pl.pallas_call(
        functools.partial(_diff_flash_body, tq=tq, tk=tk, lam_init=lam_init),
        out_shape=jax.ShapeDtypeStruct((b, t, DIFF_HEADS * DIFF_V_DIM), BF16),
        grid=(b, DIFF_KV_HEADS, t // tq),
        in_specs=[pl.BlockSpec((1, tq, 2 * LANES), lambda bi, ki, qi: (bi, qi, ki)),
                  pl.BlockSpec((1, t, LANES), lambda bi, ki, qi: (bi, 0, ki)),
                  pl.BlockSpec((per_batch, DIFF_V_DIM, chunk), lambda bi, ki, qi: (bi, ki, 0)),
                  lam_spec, lam_spec, lam_spec, lam_spec,
                  pl.BlockSpec((DIFF_V_DIM, 1), const)],
        out_specs=pl.BlockSpec((1, tq, 2 * LANES), lambda bi, ki, qi: (bi, qi, ki)),
        scratch_shapes=[pltpu.VMEM((1, 4 * tq), F32),
                        pltpu.VMEM((DIFF_V_DIM + ONES_ROWS, 4 * tq), F32),
                        pltpu.VMEM((2, tk, 4 * tq), F32)],
        compiler_params=_params(("parallel", "parallel", "arbitrary")),
        name="diff_flash",
    )(qd, kd, vdt, lq1, lk1, lq2, lk2, subln_col)


def _mla_flash_body(q_ref, kv_ref, ckvt_ref, wuvt_ref, o_ref, m_ref, acc_ref, s_ref, *, tq, tk):
    qi = pl.program_id(1)
    chunks = tk // ckvt_ref.shape[2]
    q = jnp.concatenate([q_ref[hh] for hh in range(MLA_HEADS)], axis=0)
    ones = jnp.ones((ONES_ROWS, tk), BF16)

    _init_flash_state_t(m_ref, acc_ref)

    def step(j, masked):
        start = pl.multiple_of(j * tk, tk)
        ct = jnp.concatenate([ckvt_ref[j * chunks + c] for c in range(chunks)], axis=1)
        s = s_ref[j & 1]
        if masked:
            s = jnp.where(_causal_mask_t(s.shape, start, qi * tq, tq), s, MASK_VALUE)
        else:
            nxt = pl.multiple_of((j + 1) * tk, tk)
            s_ref[(j + 1) & 1] = _dot_nt(kv_ref[0, pl.ds(nxt, tk), :], q)
        _flash_update_t(s, ct, ones, m_ref, acc_ref)

    s_ref[0] = _dot_nt(kv_ref[0, pl.ds(0, tk), :], q)
    _causal_key_loop(step, qi, tq, tk)

    acc = acc_ref[...]
    o = (acc[0:KV_LORA_RANK] / acc[KV_LORA_RANK:KV_LORA_RANK + 1]).astype(BF16)
    for hh in range(MLA_HEADS):
        out_t = _dot(wuvt_ref[hh], o[:, hh * tq:(hh + 1) * tq])
        o_ref[0, :, hh * V_HEAD_DIM:(hh + 1) * V_HEAD_DIM] = out_t.T.astype(BF16)


def _mla_flash(qm, kvm, ckvt, w_uvt, *, tq, tk):
    b, t, _ = kvm.shape
    n_chunk, _, chunk = ckvt.shape
    per_batch = n_chunk // b
    nq = t // tq
    return pl.pallas_call(
        functools.partial(_mla_flash_body, tq=tq, tk=tk),
        out_shape=jax.ShapeDtypeStruct((b, t, MLA_HEADS * V_HEAD_DIM), BF16),
        grid=(b, nq),
        in_specs=[pl.BlockSpec((MLA_HEADS, tq, MLA_KEY_COLS), lambda bi, qi: (0, bi * nq + qi, 0)),
                  pl.BlockSpec((1, t, MLA_KEY_COLS), lambda bi, qi: (bi, 0, 0)),
                  pl.BlockSpec((per_batch, KV_LORA_RANK, chunk), lambda bi, qi: (bi, 0, 0)),
                  pl.BlockSpec(w_uvt.shape, lambda bi, qi: (0, 0, 0))],
        out_specs=pl.BlockSpec((1, tq, MLA_HEADS * V_HEAD_DIM), lambda bi, qi: (bi, qi, 0)),
        scratch_shapes=[pltpu.VMEM((1, MLA_HEADS * tq), F32),
                        pltpu.VMEM((KV_LORA_RANK + ONES_ROWS, MLA_HEADS * tq), F32),
                        pltpu.VMEM((2, tk, MLA_HEADS * tq), F32)],
        compiler_params=_params(("parallel", "arbitrary")),
        name="mla_flash",
    )(qm, kvm, ckvt, w_uvt)


def _decode_body(pt_ref, qbd_ref, qm_ref, knew_ref, vnew_ref, cnew_ref, pnew_ref,
                 lq1, lk1, lq2, lk2, subln_ref, ck_hbm, cv_hbm, cc_hbm, cp_hbm, od_ref, om_ref,
                 kbuf, vbuf, cbuf, pbuf, sems, md_ref, ld_ref, accd_ref, mm_ref, lm_ref, accm_ref,
                 *, pages, lam_init):
    b = pl.program_id(0)
    step = pl.program_id(1)
    n_steps = pl.num_programs(1)
    flat = b * n_steps + step
    slot = flat & 1
    n_rows_d = 2 * DIFF_HEADS
    page = cbuf.shape[2]
    caches = ((ck_hbm, kbuf), (cv_hbm, vbuf), (cc_hbm, cbuf), (cp_hbm, pbuf))

    def page_copy(cache, page_idx, slot_, i):
        hbm, buf = caches[cache]
        return pltpu.make_async_copy(hbm.at[page_idx], buf.at[slot_, i], sems.at[slot_, cache])

    def start_fetch(b_, step_, slot_):
        for i in range(pages):
            page_idx = pt_ref[b_, step_ * pages + i]
            for cache in range(len(caches)):
                page_copy(cache, page_idx, slot_, i).start()

    def wait_fetch(slot_):
        for i in range(pages):
            for cache in range(len(caches)):
                page_copy(cache, 0, slot_, i).wait()

    @pl.when(flat == 0)
    def _():
        start_fetch(0, 0, 0)

    @pl.when(flat + 1 < pl.num_programs(0) * n_steps)
    def _():
        wraps = step + 1 == n_steps
        start_fetch(jnp.where(wraps, b + 1, b), jnp.where(wraps, 0, step + 1), 1 - slot)

    wait_fetch(slot)
    k_refs = [kbuf.at[slot, i] for i in range(pages)]
    v_refs = [vbuf.at[slot, i] for i in range(pages)]
    c_refs = [cbuf.at[slot, i] for i in range(pages)]
    p_refs = [pbuf.at[slot, i] for i in range(pages)]

    @pl.when(step == 0)
    def _():
        _init_softmax_state(md_ref, ld_ref, accd_ref)
        _init_softmax_state(mm_ref, lm_ref, accm_ref)

    qbd = qbd_ref[0]
    qm = qm_ref[0]
    q_lat = qm[:, 0:KV_LORA_RANK]
    q_pe = qm[:, KV_LORA_RANK:KV_LORA_RANK + QK_ROPE_DIM]
    row_head = (lax.broadcasted_iota(jnp.int32, (n_rows_d, DIFF_V_DIM), 0) & (DIFF_HEADS - 1)) // DIFF_GROUP

    sd = jnp.concatenate([_dot(qbd, k_refs[i][...].astype(BF16)) for i in range(pages)], axis=1)
    alpha, p = _online_update(sd, md_ref, ld_ref)
    p = p.astype(BF16)
    pv = jnp.zeros((n_rows_d, DIFF_V_DIM), F32)
    for kk in range(DIFF_KV_HEADS):
        pv_k = _dot(p[:, 0:page], v_refs[0][pl.ds(kk, page, stride=DIFF_KV_HEADS), :].astype(BF16))
        for i in range(1, pages):
            pv_k += _dot(p[:, i * page:(i + 1) * page],
                         v_refs[i][pl.ds(kk, page, stride=DIFF_KV_HEADS), :].astype(BF16))
        pv += jnp.where(row_head == kk, pv_k, 0.0)
    accd_ref[...] = alpha * accd_ref[...] + pv

    cs = [c_refs[i][...].astype(BF16) for i in range(pages)]
    sm = jnp.concatenate([_dot_nt(q_lat, cs[i]) + _dot(q_pe, p_refs[i][...].astype(BF16))
                          for i in range(pages)], axis=1)
    alpha, p = _online_update(sm, mm_ref, lm_ref)
    p = p.astype(BF16)
    pv = _dot(p[:, 0:page], cs[0])
    for i in range(1, pages):
        pv += _dot(p[:, i * page:(i + 1) * page], cs[i])
    accm_ref[...] = alpha * accm_ref[...] + pv

    @pl.when(step == pl.num_programs(1) - 1)
    def _():
        round_bf16 = lambda a: a.astype(BF16).astype(F32)
        k_new = round_bf16(knew_ref[0])
        v_new = round_bf16(vnew_ref[0])
        s_new = jnp.sum(qbd.astype(F32) * k_new, axis=-1, keepdims=True)
        alpha, p_new = _online_update(s_new, md_ref, ld_ref)
        v_own = jnp.zeros((n_rows_d, DIFF_V_DIM), F32)
        for kk in range(DIFF_KV_HEADS):
            v_own += jnp.where(row_head == kk, v_new[:, kk * DIFF_V_DIM:(kk + 1) * DIFF_V_DIM], 0.0)
        o = (alpha * accd_ref[...] + round_bf16(p_new) * v_own) / ld_ref[...]
        lam = _diff_lambda(lq1, lk1, lq2, lk2, lam_init)
        a = o[0:DIFF_HEADS] - lam * o[DIFF_HEADS:n_rows_d]
        od_ref[0] = _rms(a, subln_ref[...]) * (1.0 - lam_init)

        c_new = round_bf16(cnew_ref[0])
        pe_new = round_bf16(pnew_ref[0])
        s_new = (jnp.sum(q_lat.astype(F32) * c_new, axis=-1, keepdims=True)
                 + jnp.sum(q_pe.astype(F32) * pe_new, axis=-1, keepdims=True))
        alpha, p_new = _online_update(s_new, mm_ref, lm_ref)
        om_ref[0] = (alpha * accm_ref[...] + round_bf16(p_new) * c_new) / lm_ref[...]


def _decode(page_table, qbd, qm, k_new, v_new, c_new, pe_new, lq1, lk1, lq2, lk2, subln,
            cache_kt, cache_v, cache_c, cache_pt, *, pages, lam_init):
    nb, n_pages = page_table.shape
    per_sample = lambda b, s, pt: (b, 0, 0)
    const = lambda b, s, pt: (0, 0)
    lam_spec = pl.BlockSpec((1, DIFF_HEAD_DIM), const)
    caches = (cache_kt, cache_v, cache_c, cache_pt)
    n_rows_d = 2 * DIFF_HEADS
    grid_spec = pltpu.PrefetchScalarGridSpec(
        num_scalar_prefetch=1,
        grid=(nb, n_pages // pages),
        in_specs=[pl.BlockSpec((1, n_rows_d, DIFF_K_COLS), per_sample),
                  pl.BlockSpec((1, MLA_HEADS, MLA_KEY_COLS), per_sample),
                  pl.BlockSpec((1, 1, DIFF_K_COLS), per_sample),
                  pl.BlockSpec((1, 1, DIFF_V_COLS), per_sample),
                  pl.BlockSpec((1, 1, KV_LORA_RANK), per_sample),
                  pl.BlockSpec((1, 1, QK_ROPE_DIM), per_sample),
                  lam_spec, lam_spec, lam_spec, lam_spec,
                  pl.BlockSpec((1, DIFF_V_DIM), const)]
                 + [pl.BlockSpec(memory_space=pl.ANY) for _ in caches],
        out_specs=(pl.BlockSpec((1, DIFF_HEADS, DIFF_V_DIM), per_sample),
                   pl.BlockSpec((1, MLA_HEADS, KV_LORA_RANK), per_sample)),
        scratch_shapes=[pltpu.VMEM((2, pages) + c.shape[1:], c.dtype) for c in caches]
                       + [pltpu.SemaphoreType.DMA((2, len(caches))),
                          pltpu.VMEM((n_rows_d, 1), F32), pltpu.VMEM((n_rows_d, 1), F32),
                        pltpu.VMEM((n_rows_d, DIFF_V_DIM), F32),
                        pltpu.VMEM((MLA_HEADS, 1), F32), pltpu.VMEM((MLA_HEADS, 1), F32),
                        pltpu.VMEM((MLA_HEADS, KV_LORA_RANK), F32)],
    )
    return pl.pallas_call(
        functools.partial(_decode_body, pages=pages, lam_init=lam_init),
        out_shape=(jax.ShapeDtypeStruct((nb, DIFF_HEADS, DIFF_V_DIM), F32),
                   jax.ShapeDtypeStruct((nb, MLA_HEADS, KV_LORA_RANK), F32)),
        grid_spec=grid_spec,
        compiler_params=_params(("arbitrary", "arbitrary")),
        name="decode",
    )(page_table, qbd, qm, k_new, v_new, c_new, pe_new, lq1, lk1, lq2, lk2, subln, *caches)


def _uv_body(o_ref, wuv_ref, out_ref):
    for hh in range(MLA_HEADS):
        out_ref[:, hh * V_HEAD_DIM:(hh + 1) * V_HEAD_DIM] = _dot(
            o_ref[hh].astype(BF16), wuv_ref[hh]).astype(BF16)


def _uv(o_lat, w_uvh):
    _, m, _ = o_lat.shape
    return pl.pallas_call(
        _uv_body,
        out_shape=jax.ShapeDtypeStruct((m, MLA_HEADS * V_HEAD_DIM), BF16),
        name="uv",
    )(o_lat, w_uvh)


def _oproj_body(md_ref, mm_ref, wo_ref, x_ref, g_ref, o_ref):
    half = md_ref.shape[1]
    y = _dot(md_ref[...], wo_ref[0:half, :]) + _dot(mm_ref[...], wo_ref[half:2 * half, :])
    o_ref[...] = x_ref[...] + _rms(y, g_ref[...])


def _oproj(mixed_d, mixed_m, w_o, x, gain, *, tm):
    m, d = x.shape
    half = mixed_d.shape[1]
    row = lambda i: (i, 0)
    const = lambda i: (0, 0)
    return pl.pallas_call(
        _oproj_body,
        out_shape=jax.ShapeDtypeStruct((m, d), F32),
        grid=(m // tm,),
        in_specs=[pl.BlockSpec((tm, half), row),
                  pl.BlockSpec((tm, half), row),
                  pl.BlockSpec(w_o.shape, const),
                  pl.BlockSpec((tm, d), row),
                  pl.BlockSpec((1, d), const)],
        out_specs=pl.BlockSpec((tm, d), row),
        compiler_params=_params(("parallel",)),
        name="oproj",
    )(mixed_d, mixed_m, w_o, x, gain)


def _rope_tables(pos):
    half = DIFF_HEAD_DIM // 2
    inv = jnp.exp(-math.log(ROPE_THETA) * jnp.arange(half, dtype=F32) * (2.0 / DIFF_HEAD_DIM))
    ang = pos[:, None] * inv[None, :]
    cos, sin = jnp.cos(ang), jnp.sin(ang)
    return jnp.tile(cos, (1, 4)), jnp.concatenate([-sin, sin, -sin, sin], axis=1)


def kernel(x_prompt, x_sample, cache_diff_k, cache_diff_v, cache_mla_ckv, cache_mla_kpe, page_table, ln_ffn1_pre, ln_ffn1_post, ffn1_w_gate, ffn1_w_up, ffn1_w_down, ln_mix_pre, ln_mix_post, w_in, diff_lambda_q1, diff_lambda_k1, diff_lambda_q2, diff_lambda_k2, diff_subln, mla_q_norm, w_uq, mla_kv_norm, w_uk, w_uv, w_o, ln_ffn2_pre, ln_ffn2_post, ffn2_w_gate, ffn2_w_up, ffn2_w_down):
    bsz, t_p, d = x_prompt.shape
    nb, t_s, _ = x_sample.shape
    assert t_s == 1, "one new token per sample"
    depth = w_in.shape[0]
    n_phys, page = cache_diff_k.shape[1:3]
    past_len = page_table.shape[1] * page

    xp = x_prompt.reshape(bsz * t_p, d)
    xs = x_sample.reshape(nb * t_s, d)
    cos_p, sin_p = _rope_tables(jnp.tile(jnp.arange(t_p, dtype=F32), bsz))
    cos_s, sin_s = _rope_tables(jnp.full((nb,), past_len, dtype=F32))

    new_p = [[], [], [], []]
    new_s = [[], [], [], []]
    for l in range(depth):
        lam_init = 0.8 - 0.6 * math.exp(-0.3 * l)
        bf = lambda w: w.astype(BF16)
        ffn1 = (ln_ffn1_pre[l][None], ln_ffn1_post[l][None], ffn1_w_gate[l], ffn1_w_up[l], ffn1_w_down[l])
        ffn2 = (ln_ffn2_pre[l][None], ln_ffn2_post[l][None], ffn2_w_gate[l], ffn2_w_up[l], ffn2_w_down[l])
        w_in_p = bf(jnp.pad(w_in[l], ((0, 0), (0, LANES - QK_ROPE_DIM))))
        w_uq_h = w_uq[l].reshape(Q_LORA_RANK, MLA_HEADS, QK_NOPE_DIM + QK_ROPE_DIM)
        w_uq_p = bf(jnp.concatenate([
            w_uq_h[:, :, :QK_NOPE_DIM].reshape(Q_LORA_RANK, MLA_HEADS * QK_NOPE_DIM),
            jnp.pad(w_uq_h[:, :, QK_NOPE_DIM:], ((0, 0), (0, 0), (0, LANES - QK_ROPE_DIM))
                    ).reshape(Q_LORA_RANK, MLA_HEADS * LANES)], axis=1))
        w_ukt = bf(jnp.transpose(w_uk[l], (1, 2, 0)))
        w_uvh = bf(jnp.transpose(w_uv[l], (1, 0, 2)))
        w_uvt = bf(jnp.transpose(w_uv[l], (1, 2, 0)))
        w_o_b = bf(w_o[l])
        proj_w = (ln_mix_pre[l][None], w_in_p, mla_q_norm[l][None], w_uq_p, mla_kv_norm[l][None], w_ukt)
        lams = (diff_lambda_q1[l][None], diff_lambda_k1[l][None], diff_lambda_q2[l][None], diff_lambda_k2[l][None])
        subln = diff_subln[l][None]

        xp = _ffn(xp, *ffn1, tm=FFN_ROWS, tf=FFN_HIDDEN)
        xs = _ffn(xs, *ffn1, tm=nb, tf=FFN_HIDDEN)

        qd, kdt, kdb, vdf, vdt, ckv, ckvt, kpet, kvm, qm = _proj(xp, *proj_w, cos_p, sin_p, tm=PROJ_ROWS, seq=t_p)
        mixed_d = _diff_flash(qd.reshape(bsz, t_p, -1), kdb.reshape(bsz, t_p, -1), vdt, *lams,
                              diff_subln[l][:, None], tq=DIFF_Q_TILE, tk=KEY_TILE, lam_init=lam_init)
        mixed_m = _mla_flash(qm, kvm.reshape(bsz, t_p, -1), ckvt, w_uvt, tq=MLA_Q_TILE, tk=KEY_TILE)
        xp = _oproj(mixed_d.reshape(bsz * t_p, -1), mixed_m.reshape(bsz * t_p, -1), w_o_b, xp,
                    ln_mix_post[l][None], tm=OPROJ_ROWS)
        kd_p = jnp.transpose(kdt.reshape(bsz, DIFF_KV_HEADS, 2, DIFF_HEAD_DIM, t_p), (0, 4, 1, 2, 3))
        for acc, val in zip(new_p, (kd_p, vdf, ckv, jnp.transpose(kpet, (0, 2, 1)))):
            acc.append(val)

        qd, kdt, _, vdf, _, ckv, _, kpet, _, qm = _proj(xs, *proj_w, cos_s, sin_s, tm=nb, seq=nb)
        kdf, kpe = kdt[0].T, kpet[0].T
        q5 = qd.astype(F32).reshape(nb, DIFF_KV_HEADS, DIFF_GROUP, 2, DIFF_HEAD_DIM)
        qbd = jnp.einsum('bkgcd,kK,cC->bckgKCd', q5, jnp.eye(DIFF_KV_HEADS, dtype=F32), jnp.eye(2, dtype=F32))
        qbd = qbd.reshape(nb, 2 * DIFF_HEADS, DIFF_K_COLS).astype(BF16)
        cache_kt = jnp.transpose(cache_diff_k[l].reshape(n_phys, page, DIFF_K_COLS), (0, 2, 1))
        cache_v = cache_diff_v[l].reshape(n_phys, page * DIFF_KV_HEADS, DIFF_V_DIM)
        cache_pt = jnp.transpose(cache_mla_kpe[l], (0, 2, 1))
        o_d, o_lat = _decode(
            page_table, qbd, jnp.transpose(qm, (1, 0, 2)),
            kdf[:, None], vdf[:, None], ckv[:, None], kpe[:, None], *lams, subln,
            cache_kt, cache_v, cache_mla_ckv[l], cache_pt, pages=DECODE_PAGES, lam_init=lam_init)
        mixed_d = o_d.reshape(nb, DIFF_HEADS * DIFF_V_DIM).astype(BF16)
        mixed_m = _uv(jnp.transpose(o_lat, (1, 0, 2)), w_uvh)
        xs = _oproj(mixed_d, mixed_m, w_o_b, xs, ln_mix_post[l][None], tm=nb)
        for acc, val in zip(new_s, (kdf, vdf, ckv, kpe)):
            acc.append(val)

        xp = _ffn(xp, *ffn2, tm=FFN_ROWS, tf=FFN_HIDDEN)
        xs = _ffn(xs, *ffn2, tm=nb, tf=FFN_HIDDEN)

    kd_shape = (DIFF_KV_HEADS, 2, DIFF_HEAD_DIM)
    vd_shape = (DIFF_KV_HEADS, DIFF_V_DIM)
    stack = lambda vals, lead, tail: jnp.stack([v.reshape(*lead, *tail) for v in vals])
    lead_p, lead_s = (bsz, t_p), (nb, t_s)
    return (xp.reshape(bsz, t_p, d), xs.reshape(nb, t_s, d),
            stack(new_p[0], lead_p, kd_shape), stack(new_p[1], lead_p, vd_shape),
            stack(new_p[2], lead_p, (KV_LORA_RANK,)), stack(new_p[3], lead_p, (QK_ROPE_DIM,)),
            stack(new_s[0], lead_s, kd_shape), stack(new_s[1], lead_s, vd_shape),
            stack(new_s[2], lead_s, (KV_LORA_RANK,)), stack(new_s[3], lead_s, (QK_ROPE_DIM,)))
```
